```python
import math
import jax, jax.numpy as jnp
from jax import lax
import numpy as np

D_MODEL = 1024
BATCH = 1
SEQ = 16384
DEPTH = 2
DEC_BATCH = 32
DEC_SEQ = 16
PAST_LEN = 2048

CHUNK = 64
Q_BLOCK = 128
A_HEADS = 4
A_HEAD_DIM = 64
A_V_DIM = 2 * A_HEAD_DIM
A_WIDTH = A_HEADS * A_V_DIM
B_HEADS = 4
B_KEY_DIM = 64
B_VAL_DIM = 128
B_WIDTH = B_HEADS * B_VAL_DIM
GATE_RANK = 16
GATE_TAU = 16.0
D_FF = 2816
ROPE_THETA = 10000.0
NORM_EPS = 1e-6
SUBLN_EPS = 1e-5
IN_SIZES = (2 * A_HEADS * A_HEAD_DIM, 2 * A_HEADS * A_HEAD_DIM, A_WIDTH,
            B_HEADS * B_KEY_DIM, B_HEADS * B_KEY_DIM, B_WIDTH, B_WIDTH,
            GATE_RANK, D_MODEL, D_MODEL)
IN_WIDTH = sum(IN_SIZES)

kernel_name = 'hybrid_diffattn_gla_streaming_step'


def rms_norm(x, g, eps=NORM_EPS):
    xf = x.astype(jnp.float32)
    y = xf * lax.rsqrt(jnp.mean(xf * xf, axis=-1, keepdims=True) + eps)
    return (y * g.astype(jnp.float32)).astype(x.dtype)


def rope(x, pos):
    half = x.shape[-1] // 2
    inv = jnp.exp(-math.log(ROPE_THETA) * jnp.arange(half, dtype=jnp.float32) / half)
    ang = pos.astype(jnp.float32)[:, None] * inv[None, :]
    cos = jnp.cos(ang)[None, :, None, :]
    sin = jnp.sin(ang)[None, :, None, :]
    xf = x.astype(jnp.float32)
    x1, x2 = xf[..., :half], xf[..., half:]
    return jnp.concatenate([x1 * cos - x2 * sin, x2 * cos + x1 * sin], axis=-1).astype(x.dtype)


def swiglu(h, wg, wu, wd):
    return (jax.nn.silu(h @ wg) * (h @ wu)) @ wd


def in_projection(h, w_in, w_gate_up, b_gate):
    bn, t = h.shape[0], h.shape[1]
    z = h @ w_in
    offs, o = [], 0
    for s in IN_SIZES[:-1]:
        o += s
        offs.append(o)
    qa, ka, va, qb, kb, vb, rb, a_lr, ga, gb = jnp.split(z, offs, axis=-1)
    qa = qa.reshape(bn, t, 2 * A_HEADS, A_HEAD_DIM)
    ka = ka.reshape(bn, t, 2 * A_HEADS, A_HEAD_DIM)
    va = va.reshape(bn, t, A_HEADS, A_V_DIM)
    qb = qb.reshape(bn, t, B_HEADS, B_KEY_DIM) * (B_KEY_DIM ** -0.5)
    kb = kb.reshape(bn, t, B_HEADS, B_KEY_DIM)
    vb = vb.reshape(bn, t, B_HEADS, B_VAL_DIM)
    log_a = jax.nn.log_sigmoid((a_lr @ w_gate_up + b_gate).astype(jnp.float32)) / GATE_TAU
    log_a = log_a.reshape(bn, t, B_HEADS, B_KEY_DIM)
    return qa, ka, va, qb, kb, vb, rb, log_a, ga, gb


def diff_lambda(lam, lam_init):
    lf = lam.astype(jnp.float32)
    return jnp.exp(jnp.sum(lf[0] * lf[1])) - jnp.exp(jnp.sum(lf[2] * lf[3])) + lam_init


def diff_attend(q, k, v, q_pos, k_pos, lam):
    bn, nq, nk = q.shape[0], q.shape[1], k.shape[1]
    s = jnp.einsum('bqhd,bkhd->bhqk', q, k).astype(jnp.float32) * (A_HEAD_DIM ** -0.5)
    visible = (k_pos[None, :] // CHUNK) <= (q_pos[:, None] // CHUNK)
    s = jnp.where(visible[None, None], s, -1e30)
    p = jax.nn.softmax(s, axis=-1).reshape(bn, A_HEADS, 2, nq, nk)
    a = p[:, :, 0] - lam * p[:, :, 1]
    return jnp.einsum('bhqk,bkhe->bqhe', a.astype(v.dtype), v)


def gla_chunked(q, k, v, log_a, s0, L):
    bn, t, nh, dk = q.shape
    dv = v.shape[-1]
    n = t // L
    out_dtype = v.dtype
    f32 = jnp.float32
    q = q.astype(f32).reshape(bn, n, L, nh, dk)
    k = k.astype(f32).reshape(bn, n, L, nh, dk)
    v = v.astype(f32).reshape(bn, n, L, nh, dv)
    b = jnp.cumsum(log_a.astype(f32).reshape(bn, n, L, nh, dk), axis=2)
    b_last = b[:, :, -1]
    qe = q * jnp.exp(b)
    ke = k * jnp.exp(-b)
    causal = jnp.tril(jnp.ones((L, L), dtype=bool))
    att = jnp.where(causal, jnp.einsum('bnihd,bnjhd->bnhij', qe, ke), 0.0)
    o = jnp.einsum('bnhij,bnjhe->bnihe', att, v)
    kd = k * jnp.exp(b_last[:, :, None] - b)
    ds = jnp.einsum('bnjhd,bnjhe->bnhde', kd, v)

    def step(s, inp):
        dec, d = inp
        return dec[..., None] * s + d, s

    s_fin, s_prev = lax.scan(step, s0.astype(f32),
                             (jnp.swapaxes(jnp.exp(b_last), 0, 1), jnp.swapaxes(ds, 0, 1)))
    o = o + jnp.einsum('bnihd,nbhde->bnihe', qe, s_prev)
    return o.reshape(bn, t, nh, dv).astype(out_dtype), s_fin


def token_mix(h, p_mix, lam_init, kv_past, s0, pos0, L):
    w_in_l, w_gu_l, b_g_l, lam_l, subln_l, glan_l, w_oa_l, w_ob_l, w_o_l = p_mix
    bn, t = h.shape[0], h.shape[1]
    qa, ka, va, qb, kb, vb, rb, log_a, ga, gb = in_projection(h, w_in_l, w_gu_l, b_g_l)
    pos = pos0 + jnp.arange(t)
    qa = rope(qa, pos)
    ka = rope(ka, pos)
    lam = diff_lambda(lam_l, lam_init)
    if kv_past is None:
        nb = t // Q_BLOCK
        q_blocks = jnp.swapaxes(qa.reshape(bn, nb, Q_BLOCK, 2 * A_HEADS, A_HEAD_DIM), 0, 1)
        p_blocks = pos.reshape(nb, Q_BLOCK)
        o = lax.map(lambda a: diff_attend(a[0], ka, va, a[1], pos, lam), (q_blocks, p_blocks))
        o_a = jnp.swapaxes(o, 0, 1).reshape(bn, t, A_HEADS, A_V_DIM)
    else:
        ck, cv = kv_past
        past = ck.shape[1]
        k_all = jnp.concatenate([ck.astype(ka.dtype), ka], axis=1)
        v_all = jnp.concatenate([cv.astype(va.dtype), va], axis=1)
        o_a = diff_attend(qa, k_all, v_all, pos, jnp.arange(past + t), lam)
    o_b, s_new = gla_chunked(qb, kb, vb, log_a, s0, L)
    ya = (rms_norm(o_a, subln_l, SUBLN_EPS) * (1.0 - lam_init)).reshape(bn, t, A_WIDTH) @ w_oa_l
    yb = (rms_norm(o_b, glan_l, SUBLN_EPS).reshape(bn, t, B_WIDTH) * jax.nn.silu(rb)) @ w_ob_l
    m = jax.nn.sigmoid(ga) * ya + jax.nn.sigmoid(gb) * yb
    return m @ w_o_l, ka, va, s_new


def trunk_layer(x, norm_l, p_ffn, p_mix, lam_init, kv_past, s0, pos0, L):
    wg, wu, wd = p_ffn
    h = rms_norm(x, norm_l[0])
    x = x + 0.5 * rms_norm(swiglu(h, wg[0], wu[0], wd[0]), norm_l[1])
    h = rms_norm(x, norm_l[2])
    y, k_new, v_new, s_new = token_mix(h, p_mix, lam_init, kv_past, s0, pos0, L)
    x = x + rms_norm(y, norm_l[3])
    h = rms_norm(x, norm_l[4])
    x = x + 0.5 * rms_norm(swiglu(h, wg[1], wu[1], wd[1]), norm_l[5])
    return x, k_new, v_new, s_new


def setup_inputs(seed: int = 0) -> dict:
    key = jax.random.key(seed)
    ks = jax.random.split(key, 20)
    f32 = jnp.float32
    nrm = lambda k, shape, scale: jax.random.normal(k, shape, f32) * scale
    return {
        'x_prompt': nrm(ks[0], (BATCH, SEQ, D_MODEL), 1.0),
        'x_sample': nrm(ks[1], (DEC_BATCH, DEC_SEQ, D_MODEL), 1.0),
        'cache_k': nrm(ks[2], (DEPTH, DEC_BATCH, PAST_LEN, 2 * A_HEADS, A_HEAD_DIM), 1.0),
        'cache_v': nrm(ks[3], (DEPTH, DEC_BATCH, PAST_LEN, A_HEADS, A_V_DIM), 1.0),
        'state_gla': nrm(ks[4], (DEPTH, DEC_BATCH, B_HEADS, B_KEY_DIM, B_VAL_DIM), 0.5),
        'norm_g': 1.0 + nrm(ks[5], (DEPTH, 6, D_MODEL), 0.01),
        'ffn_w_gate': nrm(ks[6], (DEPTH, 2, D_MODEL, D_FF), D_MODEL ** -0.5),
        'ffn_w_up': nrm(ks[7], (DEPTH, 2, D_MODEL, D_FF), D_MODEL ** -0.5),
        'ffn_w_down': nrm(ks[8], (DEPTH, 2, D_FF, D_MODEL), D_FF ** -0.5),
        'w_in': nrm(ks[9], (DEPTH, D_MODEL, IN_WIDTH), D_MODEL ** -0.5),
        'w_gate_up': nrm(ks[10], (DEPTH, GATE_RANK, B_HEADS * B_KEY_DIM), GATE_RANK ** -0.5),
        'b_gate': nrm(ks[11], (DEPTH, B_HEADS * B_KEY_DIM), 0.01),
        'lambda_p': nrm(ks[12], (DEPTH, 4, A_HEAD_DIM), 0.1),
        'subln_g': 1.0 + nrm(ks[13], (DEPTH, A_V_DIM), 0.01),
        'gla_norm_g': 1.0 + nrm(ks[14], (DEPTH, B_VAL_DIM), 0.01),
        'w_out_a': nrm(ks[15], (DEPTH, A_WIDTH, D_MODEL), A_WIDTH ** -0.5),
        'w_out_b': nrm(ks[16], (DEPTH, B_WIDTH, D_MODEL), B_WIDTH ** -0.5),
        'w_out': nrm(ks[17], (DEPTH, D_MODEL, D_MODEL), D_MODEL ** -0.5),
    }


def reference(x_prompt, x_sample, cache_k, cache_v, state_gla, norm_g, ffn_w_gate, ffn_w_up,
              ffn_w_down, w_in, w_gate_up, b_gate, lambda_p, subln_g, gla_norm_g,
              w_out_a, w_out_b, w_out):
    xp, xs = x_prompt, x_sample
    past = cache_k.shape[2]
    kp_l, vp_l, sp_l, ks_l, vs_l, ss_l = [], [], [], [], [], []
    for l in range(DEPTH):
        lam_init = 0.8 - 0.6 * math.exp(-0.3 * l)
        p_mix = (w_in[l], w_gate_up[l], b_gate[l], lambda_p[l], subln_g[l], gla_norm_g[l],
                 w_out_a[l], w_out_b[l], w_out[l])
        p_ffn = (ffn_w_gate[l], ffn_w_up[l], ffn_w_down[l])
        s0_p = jnp.zeros((xp.shape[0], B_HEADS, B_KEY_DIM, B_VAL_DIM), jnp.float32)
        xp, kp, vp, sp = trunk_layer(xp, norm_g[l], p_ffn, p_mix, lam_init, None, s0_p, 0, CHUNK)
        xs, kn, vn, sn = trunk_layer(xs, norm_g[l], p_ffn, p_mix, lam_init,
                                     (cache_k[l], cache_v[l]), state_gla[l], past, xs.shape[1])
        kp_l.append(kp); vp_l.append(vp); sp_l.append(sp)
        ks_l.append(kn); vs_l.append(vn); ss_l.append(sn)
    k_prompt = jnp.stack(kp_l)
    v_prompt = jnp.stack(vp_l)
    s_prompt = jnp.stack(sp_l)
    k_sample = jnp.stack(ks_l)
    v_sample = jnp.stack(vs_l)
    s_sample = jnp.stack(ss_l)
    return (xp, xs, k_prompt, v_prompt, s_prompt, k_sample, v_sample, s_sample)
```

```python
import functools
import math

import jax
import jax.numpy as jnp
import numpy as np
from jax import lax
from jax.experimental import pallas as pl
from jax.experimental.pallas import tpu as pltpu

F32 = jnp.float32
BF16 = jnp.bfloat16

CHUNK = 64
A_HEADS = 4
A_HEAD_DIM = 64
A_V_DIM = 2 * A_HEAD_DIM
A_WIDTH = A_HEADS * A_V_DIM
B_HEADS = 4
B_KEY_DIM = 64
B_VAL_DIM = 128
B_KEYS = B_HEADS * B_KEY_DIM
B_WIDTH = B_HEADS * B_VAL_DIM
GATE_RANK = 16
GATE_TAU = 16.0
ROPE_THETA = 10000.0
NORM_EPS = 1e-6
SUBLN_EPS = 1e-5
LANES = 128
MASK_VALUE = -1e30

ROW_TILE = 512
FF_TILE = 1408
ATTN_TILE = 512
GLA_BLOCK = 512
VMEM_LIMIT = 56 * 1024 * 1024


def _rms(x, g, eps):
    return x * lax.rsqrt(jnp.mean(x * x, axis=-1, keepdims=True) + eps) * g


def _params(*sem):
    return pltpu.CompilerParams(dimension_semantics=sem, vmem_limit_bytes=VMEM_LIMIT)


def _ffn_kernel(x_ref, gpre_ref, gpost_ref, wg_ref, wu_ref, wd_ref, o_ref, h_ref, acc_ref):
    j = pl.program_id(1)

    @pl.when(j == 0)
    def _():
        h_ref[...] = _rms(x_ref[...], gpre_ref[...], NORM_EPS).astype(BF16)

    h = h_ref[...]
    g = jnp.dot(h, wg_ref[...], preferred_element_type=F32)
    u = jnp.dot(h, wu_ref[...], preferred_element_type=F32)
    a = (g * jax.nn.sigmoid(g) * u).astype(BF16)
    part = jnp.dot(a, wd_ref[...], preferred_element_type=F32)

    @pl.when(j == 0)
    def _():
        acc_ref[...] = part

    @pl.when(j > 0)
    def _():
        acc_ref[...] += part

    @pl.when(j == pl.num_programs(1) - 1)
    def _():
        o_ref[...] = x_ref[...] + 0.5 * _rms(acc_ref[...], gpost_ref[...], NORM_EPS)


def _ffn(x, g_pre, g_post, wg, wu, wd):
    t, d = x.shape
    dff = wg.shape[1]
    tm = min(ROW_TILE, t)
    tf = FF_TILE if dff % FF_TILE == 0 else dff
    return pl.pallas_call(
        _ffn_kernel,
        grid=(t // tm, dff // tf),
        in_specs=[
            pl.BlockSpec((tm, d), lambda i, j: (i, 0)),
            pl.BlockSpec((1, d), lambda i, j: (0, 0)),
            pl.BlockSpec((1, d), lambda i, j: (0, 0)),
            pl.BlockSpec((d, tf), lambda i, j: (0, j)),
            pl.BlockSpec((d, tf), lambda i, j: (0, j)),
            pl.BlockSpec((tf, d), lambda i, j: (j, 0)),
        ],
        out_specs=pl.BlockSpec((tm, d), lambda i, j: (i, 0)),
        out_shape=jax.ShapeDtypeStruct((t, d), F32),
        scratch_shapes=[pltpu.VMEM((tm, d), BF16), pltpu.VMEM((tm, d), F32)],
        compiler_params=_params("parallel", "arbitrary"),
        name="ffn",
    )(x, g_pre, g_post, wg, wu, wd)


def _rope(z, cos, sin_signed, first_half):
    rot = jnp.where(first_half, pltpu.roll(z, LANES - A_HEAD_DIM // 2, 1), pltpu.roll(z, A_HEAD_DIM // 2, 1))
    return z * cos + rot * sin_signed


def _proj_kernel(x_ref, g_ref, w_ref, wgu_ref, bg_ref, cos_ref, sin_ref,
                 kf_ref, vf_ref, q1_ref, q2_ref, kb_ref, vt_ref,
                 qg_ref, kg_ref, vg_ref, rg_ref, la_ref):
    h = _rms(x_ref[...], g_ref[...], NORM_EPS).astype(BF16)
    z = jnp.dot(h, w_ref[...], preferred_element_type=F32)
    cos = cos_ref[...]
    sin = sin_ref[...]
    lane = lax.broadcasted_iota(jnp.int32, cos.shape, 1)
    first_half = (lane % A_HEAD_DIM) < (A_HEAD_DIM // 2)
    even_head = lane < A_HEAD_DIM
    scale = A_HEAD_DIM ** -0.5
    o = 0
    for hh in range(A_HEADS):
        sl = slice(hh * LANES, (hh + 1) * LANES)
        q = _rope(z[:, o + hh * LANES:o + (hh + 1) * LANES], cos, sin, first_half) * scale
        q1_ref[:, sl] = jnp.where(even_head, q, 0.0).astype(BF16)
        q2_ref[:, sl] = jnp.where(even_head, 0.0, q).astype(BF16)
    o += 2 * A_HEADS * A_HEAD_DIM
    for hh in range(A_HEADS):
        sl = slice(hh * LANES, (hh + 1) * LANES)
        k = _rope(z[:, o + hh * LANES:o + (hh + 1) * LANES], cos, sin, first_half)
        kf_ref[:, sl] = k
        kb_ref[:, sl] = k.astype(BF16)
    o += 2 * A_HEADS * A_HEAD_DIM
    va = z[:, o:o + A_WIDTH]
    vf_ref[...] = va
    vt_ref[...] = va.T.astype(BF16)
    o += A_WIDTH
    qg_ref[...] = z[:, o:o + B_KEYS] * (B_KEY_DIM ** -0.5)
    o += B_KEYS
    kg_ref[...] = z[:, o:o + B_KEYS]
    o += B_KEYS
    vg_ref[...] = z[:, o:o + B_WIDTH].astype(BF16)
    o += B_WIDTH
    rg_ref[...] = z[:, o:o + B_WIDTH]
    o += B_WIDTH
    code = z[:, o:o + LANES].astype(BF16)
    gate = jnp.dot(code, wgu_ref[...], preferred_element_type=F32) + bg_ref[...]
    la_ref[...] = (jnp.minimum(gate, 0.0) - jnp.log1p(jnp.exp(-jnp.abs(gate)))) * (1.0 / GATE_TAU)


def _proj(x, g, w, wgu, bg, cos, sin):
    t, d = x.shape
    tm = min(ROW_TILE, t)
    row = lambda n: pl.BlockSpec((tm, n), lambda i: (i, 0))
    full = lambda a: pl.BlockSpec(a.shape, lambda i: (0, 0))
    sds = jax.ShapeDtypeStruct
    return pl.pallas_call(
        _proj_kernel,
        grid=(t // tm,),
        in_specs=[row(d), full(g), full(w), full(wgu), full(bg), row(LANES), row(LANES)],
        out_specs=[row(A_WIDTH), row(A_WIDTH), row(A_WIDTH), row(A_WIDTH), row(A_WIDTH),
                   pl.BlockSpec((A_WIDTH, tm), lambda i: (0, i)),
                   row(B_KEYS), row(B_KEYS), row(B_WIDTH), row(B_WIDTH), row(B_KEYS)],
        out_shape=[sds((t, A_WIDTH), F32), sds((t, A_WIDTH), F32),
                   sds((t, A_WIDTH), BF16), sds((t, A_WIDTH), BF16), sds((t, A_WIDTH), BF16),
                   sds((A_WIDTH, t), BF16),
                   sds((t, B_KEYS), F32), sds((t, B_KEYS), F32), sds((t, B_WIDTH), BF16),
                   sds((t, B_WIDTH), F32), sds((t, B_KEYS), F32)],
        compiler_params=_params("parallel"),
        name="mixer_proj",
    )(x, g, w, wgu, bg, cos, sin)


def _diff_lambda(lp, lam_init):
    a = jnp.sum(lp[0:1] * lp[1:2], axis=-1, keepdims=True)
    b = jnp.sum(lp[2:3] * lp[3:4], axis=-1, keepdims=True)
    return jnp.exp(a) - jnp.exp(b) + lam_init


def _subln(o, g, lam_init):
    return _rms(o, g, SUBLN_EPS) * (1.0 - lam_init)


def _attn_prompt_kernel(qi_ref, kj_ref, fl_ref, q1_ref, q2_ref, k_ref, vt_ref, lp_ref, g_ref,
                        o_ref, m_ref, l_ref, acc_ref, *, lam_init):
    n = pl.program_id(0)
    tk = k_ref.shape[0]
    tq = q1_ref.shape[0]
    flags = fl_ref[n]

    @pl.when((flags & 1) != 0)
    def _():
        m_ref[...] = jnp.full(m_ref.shape, MASK_VALUE, F32)
        l_ref[...] = jnp.zeros(l_ref.shape, F32)
        acc_ref[...] = jnp.zeros(acc_ref.shape, F32)

    def step(masked):
        if masked:
            kpos = kj_ref[n] * tk + lax.broadcasted_iota(jnp.int32, (tk, tq), 0)
            qpos = qi_ref[n] * tq + lax.broadcasted_iota(jnp.int32, (tk, tq), 1)
            visible = (kpos // CHUNK) <= (qpos // CHUNK)
        for hh in range(A_HEADS):
            sl = slice(hh * LANES, (hh + 1) * LANES)
            kh = k_ref[:, sl]
            vth = vt_ref[sl, :]
            for c, q_ref in enumerate((q1_ref, q2_ref)):
                idx = 2 * hh + c
                st = lax.dot_general(kh, q_ref[:, sl], (((1,), (1,)), ((), ())),
                                     preferred_element_type=F32)
                if masked:
                    st = jnp.where(visible, st, MASK_VALUE)
                m_old = m_ref[idx]
                m_new = jnp.maximum(m_old, jnp.max(st, axis=0, keepdims=True))
                alpha = jnp.exp(m_old - m_new)
                p = jnp.exp(st - m_new)
                l_ref[idx] = alpha * l_ref[idx] + jnp.sum(p, axis=0, keepdims=True)
                m_ref[idx] = m_new
                pv = jnp.dot(vth, p.astype(BF16), preferred_element_type=F32)
                acc_ref[idx] = alpha * acc_ref[idx] + pv

    @pl.when((flags & 2) == 0)
    def _():
        step(False)

    @pl.when((flags & 2) != 0)
    def _():
        step(True)

    @pl.when((flags & 4) != 0)
    def _():
        lam = _diff_lambda(lp_ref[...], lam_init)
        for hh in range(A_HEADS):
            o1 = acc_ref[2 * hh] * (1.0 / l_ref[2 * hh])
            o2 = acc_ref[2 * hh + 1] * (1.0 / l_ref[2 * hh + 1])
            o = (o1 - lam * o2).T
            o_ref[:, hh * LANES:(hh + 1) * LANES] = _subln(o, g_ref[...], lam_init).astype(BF16)


def _attn_pairs(t, tq, tk):
    qi, kj, fl = [], [], []
    for i in range(t // tq):
        q_lo, q_hi = (i * tq) // CHUNK, (i * tq + tq - 1) // CHUNK
        js = [j for j in range(t // tk) if (j * tk) // CHUNK <= q_hi]
        for j in js:
            partial = (j * tk + tk - 1) // CHUNK > q_lo
            qi.append(i)
            kj.append(j)
            fl.append((1 if j == js[0] else 0) | (2 if partial else 0) | (4 if j == js[-1] else 0))
    return (np.asarray(qi, np.int32), np.asarray(kj, np.int32), np.asarray(fl, np.int32))


def _attn_prompt(q1, q2, kb, vt, lam_p, subln_g, lam_init):
    t = q1.shape[0]
    tq = tk = min(ATTN_TILE, t)
    qi, kj, fl = _attn_pairs(t, tq, tk)
    grid_spec = pltpu.PrefetchScalarGridSpec(
        num_scalar_prefetch=3,
        grid=(len(qi),),
        in_specs=[
            pl.BlockSpec((tq, A_WIDTH), lambda n, qi, kj, fl: (qi[n], 0)),
            pl.BlockSpec((tq, A_WIDTH), lambda n, qi, kj, fl: (qi[n], 0)),
            pl.BlockSpec((tk, A_WIDTH), lambda n, qi, kj, fl: (kj[n], 0)),
            pl.BlockSpec((A_WIDTH, tk), lambda n, qi, kj, fl: (0, kj[n])),
            pl.BlockSpec(lam_p.shape, lambda n, qi, kj, fl: (0, 0)),
            pl.BlockSpec(subln_g.shape, lambda n, qi, kj, fl: (0, 0)),
        ],
        out_specs=pl.BlockSpec((tq, A_WIDTH), lambda n, qi, kj, fl: (qi[n], 0)),
        scratch_shapes=[pltpu.VMEM((2 * A_HEADS, 1, tq), F32), pltpu.VMEM((2 * A_HEADS, 1, tq), F32),
                        pltpu.VMEM((2 * A_HEADS, A_V_DIM, tq), F32)],
    )
    return pl.pallas_call(
        functools.partial(_attn_prompt_kernel, lam_init=lam_init),
        grid_spec=grid_spec,
        out_shape=jax.ShapeDtypeStruct((t, A_WIDTH), BF16),
        compiler_params=_params("arbitrary"),
        name="attn_prompt",
    )(jnp.asarray(qi), jnp.asarray(kj), jnp.asarray(fl), q1, q2, kb, vt, lam_p, subln_g)


def _attn_sample_kernel(q1_ref, q2_ref, kn_ref, vn_ref, ck_ref, cv_ref, lp_ref, g_ref, o_ref, *, lam_init):
    past = ck_ref.shape[0]
    tq = q1_ref.shape[0]
    qpos = past + lax.broadcasted_iota(jnp.int32, (tq, past), 0)
    vis_past = (lax.broadcasted_iota(jnp.int32, (tq, past), 1) // CHUNK) <= (qpos // CHUNK)
    qpos_n = lax.broadcasted_iota(jnp.int32, (tq, tq), 0)
    vis_new = (lax.broadcasted_iota(jnp.int32, (tq, tq), 1) + past) // CHUNK <= (qpos_n + past) // CHUNK
    lam = _diff_lambda(lp_ref[...], lam_init)
    nt = (((1,), (1,)), ((), ()))
    for hh in range(A_HEADS):
        sl = slice(hh * LANES, (hh + 1) * LANES)
        ck = ck_ref[:, sl].astype(BF16)
        cv = cv_ref[:, sl].astype(BF16)
        kn = kn_ref[:, sl]
        vn = vn_ref[:, sl].astype(BF16)
        outs = []
        for q_ref in (q1_ref, q2_ref):
            q = q_ref[:, sl]
            s_p = jnp.where(vis_past, lax.dot_general(q, ck, nt, preferred_element_type=F32), MASK_VALUE)
            s_n = jnp.where(vis_new, lax.dot_general(q, kn, nt, preferred_element_type=F32), MASK_VALUE)
            m = jnp.maximum(jnp.max(s_p, axis=-1, keepdims=True), jnp.max(s_n, axis=-1, keepdims=True))
            p_p = jnp.exp(s_p - m)
            p_n = jnp.exp(s_n - m)
            l = jnp.sum(p_p, axis=-1, keepdims=True) + jnp.sum(p_n, axis=-1, keepdims=True)
            pv = (jnp.dot(p_p.astype(BF16), cv, preferred_element_type=F32)
                  + jnp.dot(p_n.astype(BF16), vn, preferred_element_type=F32))
            outs.append(pv * (1.0 / l))
        o = outs[0] - lam * outs[1]
        o_ref[:, sl] = _subln(o, g_ref[...], lam_init).astype(BF16)


def _attn_sample(q1, q2, kb, vf, cache_k, cache_v, lam_p, subln_g, lam_init):
    nb, past, width = cache_k.shape
    tq = q1.shape[0] // nb
    new = lambda: pl.BlockSpec((tq, width), lambda b: (b, 0))
    old = lambda: pl.BlockSpec((None, past, width), lambda b: (b, 0, 0))
    return pl.pallas_call(
        functools.partial(_attn_sample_kernel, lam_init=lam_init),
        grid=(nb,),
        in_specs=[new(), new(), new(), new(), old(), old(),
                  pl.BlockSpec(lam_p.shape, lambda b: (0, 0)), pl.BlockSpec(subln_g.shape, lambda b: (0, 0))],
        out_specs=new(),
        out_shape=jax.ShapeDtypeStruct((nb * tq, width), BF16),
        compiler_params=_params("parallel"),
        name="attn_sample",
    )(q1, q2, kb, vf, cache_k, cache_v, lam_p, subln_g)


def _gla_kernel(q_ref, k_ref, v_ref, la_ref, r_ref, s0_ref, g_ref, o_ref, s_ref, st_ref, *, chunk):
    blk = pl.program_id(1)
    tokens = q_ref.shape[0]

    @pl.when(blk == 0)
    def _():
        st_ref[...] = s0_ref[...].T

    row = lax.broadcasted_iota(jnp.int32, (chunk, chunk), 0)
    col = lax.broadcasted_iota(jnp.int32, (chunk, chunk), 1)
    causal = col <= row
    tri = jnp.where(causal, 1.0, 0.0).astype(BF16)
    lane = lax.broadcasted_iota(jnp.int32, (chunk, B_KEYS), 1)
    nt = (((1,), (1,)), ((), ()))
    tn = (((0,), (0,)), ((), ()))
    for c in range(tokens // chunk):
        r = slice(c * chunk, (c + 1) * chunk)
        la = la_ref[r, :]
        la_hi = la.astype(BF16)
        la_lo = (la - la_hi.astype(F32)).astype(BF16)
        b = (jnp.dot(tri, la_hi, preferred_element_type=F32)
             + jnp.dot(tri, la_lo, preferred_element_type=F32))
        b_last = b[chunk - 1:chunk, :]
        qe = q_ref[r, :] * jnp.exp(b)
        k = k_ref[r, :]
        ke = (k * jnp.exp(-b)).astype(BF16)
        kd = k * jnp.exp(b_last - b)
        st = st_ref[...]
        st_bf = st.astype(BF16)
        st_new = st * jnp.exp(b_last)
        for hh in range(B_HEADS):
            in_head = (lane // B_KEY_DIM) == hh
            vs = slice(hh * B_VAL_DIM, (hh + 1) * B_VAL_DIM)
            v = v_ref[r, vs]
            qe_h = jnp.where(in_head, qe, 0.0).astype(BF16)
            att = lax.dot_general(qe_h, ke, nt, preferred_element_type=F32)
            att = jnp.where(causal, att, 0.0).astype(BF16)
            o = (jnp.dot(att, v, preferred_element_type=F32)
                 + lax.dot_general(qe_h, st_bf, nt, preferred_element_type=F32))
            rg = r_ref[r, vs]
            o_ref[r, vs] = (_rms(o, g_ref[...], SUBLN_EPS) * (rg * jax.nn.sigmoid(rg))).astype(BF16)
            kd_h = jnp.where(in_head, kd, 0.0).astype(BF16)
            st_new = st_new + lax.dot_general(v, kd_h, tn, preferred_element_type=F32)
        st_ref[...] = st_new

    @pl.when(blk == pl.num_programs(1) - 1)
    def _():
        s_ref[...] = st_ref[...].T


def _gla(q, k, v, la, rg, s0, g, nb, chunk):
    t = q.shape[0] // nb
    tb = min(GLA_BLOCK, t)
    nblk = t // tb
    tok = lambda n: pl.BlockSpec((tb, n), lambda b, i: (b * nblk + i, 0))
    state = pl.BlockSpec((None, B_KEYS, B_VAL_DIM), lambda b, i: (b, 0, 0))
    return pl.pallas_call(
        functools.partial(_gla_kernel, chunk=chunk),
        grid=(nb, nblk),
        in_specs=[tok(B_KEYS), tok(B_KEYS), tok(B_WIDTH), tok(B_KEYS), tok(B_WIDTH), state,
                  pl.BlockSpec(g.shape, lambda b, i: (0, 0))],
        out_specs=[tok(B_WIDTH), state],
        out_shape=[jax.ShapeDtypeStruct((nb * t, B_WIDTH), BF16),
                   jax.ShapeDtypeStruct((nb, B_KEYS, B_VAL_DIM), F32)],
        scratch_shapes=[pltpu.VMEM((B_VAL_DIM, B_KEYS), F32)],
        compiler_params=_params("parallel", "arbitrary"),
        name="gla",
    )(q, k, v, la, rg, s0, g)


def _merge_kernel(x_ref, oa_ref, ob_ref, gpre_ref, gpost_ref, wgab_ref, woa_ref, wob_ref, wo_ref, o_ref):
    x = x_ref[...]
    d = x.shape[1]
    h = _rms(x, gpre_ref[...], NORM_EPS).astype(BF16)
    gab = jnp.dot(h, wgab_ref[...], preferred_element_type=F32)
    ya = jnp.dot(oa_ref[...], woa_ref[...], preferred_element_type=F32)
    yb = jnp.dot(ob_ref[...], wob_ref[...], preferred_element_type=F32)
    m = jax.nn.sigmoid(gab[:, :d]) * ya + jax.nn.sigmoid(gab[:, d:]) * yb
    y = jnp.dot(m.astype(BF16), wo_ref[...], preferred_element_type=F32)
    o_ref[...] = x + _rms(y, gpost_ref[...], NORM_EPS)


def _merge(x, oa, ob, g_pre, g_post, wgab, woa, wob, wo):
    t, d = x.shape
    tm = min(ROW_TILE, t)
    row = lambda n: pl.BlockSpec((tm, n), lambda i: (i, 0))
    full = lambda a: pl.BlockSpec(a.shape, lambda i: (0, 0))
    return pl.pallas_call(
        _merge_kernel,
        grid=(t // tm,),
        in_specs=[row(d), row(A_WIDTH), row(B_WIDTH), full(g_pre), full(g_post),
                  full(wgab), full(woa), full(wob), full(wo)],
        out_specs=row(d),
        out_shape=jax.ShapeDtypeStruct((t, d), F32),
        compiler_params=_params("parallel"),
        name="mixer_merge",
    )(x, oa, ob, g_pre, g_post, wgab, woa, wob, wo)


def _rope_tables(pos):
    half = A_HEAD_DIM // 2
    inv = jnp.exp(-math.log(ROPE_THETA) * jnp.arange(half, dtype=F32) / half)
    ang = pos.astype(F32)[:, None] * inv[None, :]
    cos, sin = jnp.cos(ang), jnp.sin(ang)
    reps = LANES // A_HEAD_DIM
    return (jnp.tile(jnp.concatenate([cos, cos], axis=-1), (1, reps)),
            jnp.tile(jnp.concatenate([-sin, sin], axis=-1), (1, reps)))


def kernel(x_prompt, x_sample, cache_k, cache_v, state_gla, norm_g, ffn_w_gate, ffn_w_up, ffn_w_down, w_in,
           w_gate_up, b_gate, lambda_p, subln_g, gla_norm_g, w_out_a, w_out_b, w_out):
    depth = w_in.shape[0]
    bp, tp, d = x_prompt.shape
    bs, ts, _ = x_sample.shape
    past = cache_k.shape[2]
    assert bp == 1 and tp % CHUNK == 0

    n_qk = 2 * A_HEADS * A_HEAD_DIM
    sizes = (n_qk, n_qk, A_WIDTH, B_KEYS, B_KEYS, B_WIDTH, B_WIDTH, GATE_RANK, d, d)
    offs = np.concatenate([[0], np.cumsum(sizes)])
    assert w_in.shape[2] == offs[-1]
    code_lo, code_hi = int(offs[7]), int(offs[8])

    cos_p, sin_p = _rope_tables(jnp.arange(tp))
    cos_s, sin_s = _rope_tables(jnp.tile(past + jnp.arange(ts), bs))

    xp = x_prompt.reshape(tp, d)
    xs = x_sample.reshape(bs * ts, d)
    outs = [[] for _ in range(6)]
    for l in range(depth):
        lam_init = 0.8 - 0.6 * math.exp(-0.3 * l)
        norm = [norm_g[l, i][None, :] for i in range(6)]
        wg = ffn_w_gate[l].astype(BF16)
        wu = ffn_w_up[l].astype(BF16)
        wd = ffn_w_down[l].astype(BF16)
        w_l = w_in[l]
        w_code = jnp.pad(w_l[:, code_lo:code_hi], ((0, 0), (0, LANES - GATE_RANK)))
        w_proj = jnp.concatenate([w_l[:, :code_lo], w_code], axis=1).astype(BF16)
        w_gab = w_l[:, code_hi:].astype(BF16)
        w_gu = jnp.pad(w_gate_up[l], ((0, LANES - GATE_RANK), (0, 0))).astype(BF16)
        bg = b_gate[l][None, :]
        woa = w_out_a[l].astype(BF16)
        wob = w_out_b[l].astype(BF16)
        wo = w_out[l].astype(BF16)
        sg = subln_g[l][None, :]
        gg = gla_norm_g[l][None, :]
        lam_p = lambda_p[l]

        def layer(x, cos, sin, nb, chunk, s0, kv_past):
            x = _ffn(x, norm[0], norm[1], wg[0], wu[0], wd[0])
            kf, vf, q1, q2, kb, vt, qg, kg, vg, rg, la = _proj(x, norm[2], w_proj, w_gu, bg, cos, sin)
            if kv_past is None:
                oa = _attn_prompt(q1, q2, kb, vt, lam_p, sg, lam_init)
            else:
                oa = _attn_sample(q1, q2, kb, vf, kv_past[0], kv_past[1], lam_p, sg, lam_init)
            ob, s_new = _gla(qg, kg, vg, la, rg, s0, gg, nb, chunk)
            x = _merge(x, oa, ob, norm[2], norm[3], w_gab, woa, wob, wo)
            x = _ffn(x, norm[4], norm[5], wg[1], wu[1], wd[1])
            return x, kf, vf, s_new

        s0_p = jnp.zeros((bp, B_KEYS, B_VAL_DIM), F32)
        xp, kp, vp, sp = layer(xp, cos_p, sin_p, bp, CHUNK, s0_p, None)
        s0_s = state_gla[l].reshape(bs, B_KEYS, B_VAL_DIM)
        ck = cache_k[l].reshape(bs, past, n_qk)
        cv = cache_v[l].reshape(bs, past, A_WIDTH)
        xs, kn, vn, sn = layer(xs, cos_s, sin_s, bs, ts, s0_s, (ck, cv))
        for lst, val in zip(outs, (kp, vp, sp, kn, vn, sn)):
            lst.append(val)

    k_prompt = jnp.stack(outs[0]).reshape(depth, bp, tp, 2 * A_HEADS, A_HEAD_DIM)
    v_prompt = jnp.stack(outs[1]).reshape(depth, bp, tp, A_HEADS, A_V_DIM)
    s_prompt = jnp.stack(outs[2]).reshape(depth, bp, B_HEADS, B_KEY_DIM, B_VAL_DIM)
    k_sample = jnp.stack(outs[3]).reshape(depth, bs, ts, 2 * A_HEADS, A_HEAD_DIM)
    v_sample = jnp.stack(outs[4]).reshape(depth, bs, ts, A_HEADS, A_V_DIM)
    s_sample = jnp.stack(outs[5]).reshape(depth, bs, B_HEADS, B_KEY_DIM, B_VAL_DIM)
    return (xp.reshape(bp, tp, d), xs.reshape(bs, ts, d), k_prompt, v_prompt, s_prompt,
            k_sample, v_sample, s_sample)
```

```python
import functools
import math

import jax
import jax.numpy as jnp
import numpy as np
from jax import lax
from jax.experimental import pallas as pl
from jax.experimental.pallas import tpu as pltpu

F32 = jnp.float32
BF16 = jnp.bfloat16

CHUNK = 64
A_HEADS = 4
A_HEAD_DIM = 64
A_V_DIM = 2 * A_HEAD_DIM
A_WIDTH = A_HEADS * A_V_DIM
B_HEADS = 4
B_KEY_DIM = 64
B_VAL_DIM = 128
B_KEYS = B_HEADS * B_KEY_DIM
B_WIDTH = B_HEADS * B_VAL_DIM
GATE_RANK = 16
GATE_TAU = 16.0
ROPE_THETA = 10000.0
NORM_EPS = 1e-6
SUBLN_EPS = 1e-5
LANES = 128
MASK_VALUE = -1e30

ROW_TILE = 512
FF_TILE = 1408
ATTN_Q_TILE = 512
ATTN_K_TILE = 1024
ATTN_SCORE_SLOTS = 3
V_ROWS = A_V_DIM + 16
LOG2E = math.log2(math.e)
GLA_BLOCK = 512
VMEM_LIMIT = 56 * 1024 * 1024


def _rms(x, g, eps):
    return x * lax.rsqrt(jnp.mean(x * x, axis=-1, keepdims=True) + eps) * g


def _params(*sem):
    return pltpu.CompilerParams(dimension_semantics=sem, vmem_limit_bytes=VMEM_LIMIT)


def _ffn_kernel(x_ref, gpre_ref, gpost_ref, wg_ref, wu_ref, wd_ref, o_ref, h_ref, acc_ref):
    j = pl.program_id(1)

    @pl.when(j == 0)
    def _():
        h_ref[...] = _rms(x_ref[...], gpre_ref[...], NORM_EPS).astype(BF16)

    h = h_ref[...]
    g = jnp.dot(h, wg_ref[...], preferred_element_type=F32)
    u = jnp.dot(h, wu_ref[...], preferred_element_type=F32)
    a = (g * jax.nn.sigmoid(g) * u).astype(BF16)
    part = jnp.dot(a, wd_ref[...], preferred_element_type=F32)

    @pl.when(j == 0)
    def _():
        acc_ref[...] = part

    @pl.when(j > 0)
    def _():
        acc_ref[...] += part

    @pl.when(j == pl.num_programs(1) - 1)
    def _():
        o_ref[...] = x_ref[...] + 0.5 * _rms(acc_ref[...], gpost_ref[...], NORM_EPS)


def _ffn(x, g_pre, g_post, wg, wu, wd):
    t, d = x.shape
    dff = wg.shape[1]
    tm = min(ROW_TILE, t)
    tf = FF_TILE if dff % FF_TILE == 0 else dff
    return pl.pallas_call(
        _ffn_kernel,
        grid=(t // tm, dff // tf),
        in_specs=[
            pl.BlockSpec((tm, d), lambda i, j: (i, 0)),
            pl.BlockSpec((1, d), lambda i, j: (0, 0)),
            pl.BlockSpec((1, d), lambda i, j: (0, 0)),
            pl.BlockSpec((d, tf), lambda i, j: (0, j)),
            pl.BlockSpec((d, tf), lambda i, j: (0, j)),
            pl.BlockSpec((tf, d), lambda i, j: (j, 0)),
        ],
        out_specs=pl.BlockSpec((tm, d), lambda i, j: (i, 0)),
        out_shape=jax.ShapeDtypeStruct((t, d), F32),
        scratch_shapes=[pltpu.VMEM((tm, d), BF16), pltpu.VMEM((tm, d), F32)],
        compiler_params=_params("parallel", "arbitrary"),
        name="ffn",
    )(x, g_pre, g_post, wg, wu, wd)


def _rope(z, cos, sin_signed, first_half):
    rot = jnp.where(first_half, pltpu.roll(z, LANES - A_HEAD_DIM // 2, 1), pltpu.roll(z, A_HEAD_DIM // 2, 1))
    return z * cos + rot * sin_signed


def _proj_kernel(x_ref, g_ref, w_ref, wgu_ref, bg_ref, cos_ref, sin_ref,
                 kf_ref, vf_ref, q1_ref, q2_ref, kb_ref, vt_ref,
                 qg_ref, kg_ref, vg_ref, rg_ref, la_ref, *, k_transposed):
    h = _rms(x_ref[...], g_ref[...], NORM_EPS).astype(BF16)
    z = jnp.dot(h, w_ref[...], preferred_element_type=F32)
    cos = cos_ref[...]
    sin = sin_ref[...]
    lane = lax.broadcasted_iota(jnp.int32, cos.shape, 1)
    first_half = (lane % A_HEAD_DIM) < (A_HEAD_DIM // 2)
    even_head = lane < A_HEAD_DIM
    scale = A_HEAD_DIM ** -0.5 * LOG2E
    o = 0
    for hh in range(A_HEADS):
        sl = slice(hh * LANES, (hh + 1) * LANES)
        q = _rope(z[:, o + hh * LANES:o + (hh + 1) * LANES], cos, sin, first_half) * scale
        q1_ref[:, sl] = jnp.where(even_head, q, 0.0).astype(BF16)
        q2_ref[:, sl] = jnp.where(even_head, 0.0, q).astype(BF16)
    o += 2 * A_HEADS * A_HEAD_DIM
    for hh in range(A_HEADS):
        sl = slice(hh * LANES, (hh + 1) * LANES)
        k = _rope(z[:, o + hh * LANES:o + (hh + 1) * LANES], cos, sin, first_half)
        if k_transposed:
            kf_ref[sl, :] = k.T
        else:
            kf_ref[:, sl] = k
        kb_ref[:, sl] = k.astype(BF16)
    o += 2 * A_HEADS * A_HEAD_DIM
    va = z[:, o:o + A_WIDTH]
    vf_ref[...] = va
    for hh in range(A_HEADS):
        vt_ref[hh * V_ROWS:hh * V_ROWS + A_V_DIM, :] = va[:, hh * A_V_DIM:(hh + 1) * A_V_DIM].T.astype(BF16)
        vt_ref[hh * V_ROWS + A_V_DIM:(hh + 1) * V_ROWS, :] = jnp.ones((V_ROWS - A_V_DIM, va.shape[0]), BF16)
    o += A_WIDTH
    qg_ref[...] = z[:, o:o + B_KEYS] * (B_KEY_DIM ** -0.5)
    o += B_KEYS
    kg_ref[...] = z[:, o:o + B_KEYS]
    o += B_KEYS
    vg_ref[...] = z[:, o:o + B_WIDTH].astype(BF16)
    o += B_WIDTH
    rg_ref[...] = z[:, o:o + B_WIDTH]
    o += B_WIDTH
    code = z[:, o:o + LANES].astype(BF16)
    gate = jnp.dot(code, wgu_ref[...], preferred_element_type=F32) + bg_ref[...]
    la_ref[...] = (jnp.minimum(gate, 0.0) - jnp.log1p(jnp.exp(-jnp.abs(gate)))) * (1.0 / GATE_TAU)


def _proj(x, g, w, wgu, bg, cos, sin, k_transposed):
    t, d = x.shape
    tm = min(ROW_TILE, t)
    row = lambda n: pl.BlockSpec((tm, n), lambda i: (i, 0))
    col = lambda n: pl.BlockSpec((n, tm), lambda i: (0, i))
    full = lambda a: pl.BlockSpec(a.shape, lambda i: (0, 0))
    sds = jax.ShapeDtypeStruct
    n_qk = 2 * A_HEADS * A_HEAD_DIM
    return pl.pallas_call(
        functools.partial(_proj_kernel, k_transposed=k_transposed),
        grid=(t // tm,),
        in_specs=[row(d), full(g), full(w), full(wgu), full(bg), row(LANES), row(LANES)],
        out_specs=[col(n_qk) if k_transposed else row(n_qk),
                   row(A_WIDTH), row(A_WIDTH), row(A_WIDTH), row(A_WIDTH), col(A_HEADS * V_ROWS),
                   row(B_KEYS), row(B_KEYS), row(B_WIDTH), row(B_WIDTH), row(B_KEYS)],
        out_shape=[sds((n_qk, t) if k_transposed else (t, n_qk), F32), sds((t, A_WIDTH), F32),
                   sds((t, A_WIDTH), BF16), sds((t, A_WIDTH), BF16), sds((t, A_WIDTH), BF16),
                   sds((A_HEADS * V_ROWS, t), BF16),
                   sds((t, B_KEYS), F32), sds((t, B_KEYS), F32), sds((t, B_WIDTH), BF16),
                   sds((t, B_WIDTH), F32), sds((t, B_KEYS), F32)],
        compiler_params=_params("parallel"),
        name="mixer_proj",
    )(x, g, w, wgu, bg, cos, sin)


def _diff_lambda(lp, lam_init):
    a = jnp.sum(lp[0:1] * lp[1:2], axis=-1, keepdims=True)
    b = jnp.sum(lp[2:3] * lp[3:4], axis=-1, keepdims=True)
    return jnp.exp(a) - jnp.exp(b) + lam_init


def _subln(o, g, lam_init):
    return _rms(o, g, SUBLN_EPS) * (1.0 - lam_init)


def _attn_prompt_kernel(qi_ref, kj_ref, fl_ref, q1_ref, q2_ref, k_ref, vt_ref, lp_ref, g_ref,
                        o_ref, m_ref, acc_ref, st_ref, *, lam_init):
    n = pl.program_id(0)
    tk = k_ref.shape[0]
    flags = fl_ref[n]
    n_maps = 2 * A_HEADS
    slots = st_ref.shape[0]
    nt = (((1,), (1,)), ((), ()))

    @pl.when((flags & 1) != 0)
    def _():
        m_ref[...] = jnp.full(m_ref.shape, MASK_VALUE, F32)
        acc_ref[...] = jnp.zeros(acc_ref.shape, F32)

    def step(masked):
        if masked:
            shape = st_ref.shape[1:]
            kpos = kj_ref[n] * tk + lax.broadcasted_iota(jnp.int32, shape, 0)
            qpos = qi_ref[n] * shape[1] + lax.broadcasted_iota(jnp.int32, shape, 1)
            visible = (kpos // CHUNK) <= (qpos // CHUNK)

        def scores(idx):
            hh, c = divmod(idx, 2)
            sl = slice(hh * LANES, (hh + 1) * LANES)
            st = lax.dot_general(k_ref[:, sl], (q1_ref, q2_ref)[c][:, sl], nt, preferred_element_type=F32)
            if masked:
                st = jnp.where(visible, st, MASK_VALUE)
            st_ref[idx % slots] = st

        def accumulate(idx):
            hh = idx // 2
            st = st_ref[idx % slots]
            m_old = m_ref[idx]
            m_new = jnp.maximum(m_old, jnp.max(st, axis=0, keepdims=True))
            m_ref[idx] = m_new
            p = jnp.exp2(st - m_new).astype(BF16)
            pv = jnp.dot(vt_ref[hh * V_ROWS:(hh + 1) * V_ROWS, :], p, preferred_element_type=F32)
            acc_ref[idx] = jnp.exp2(m_old - m_new) * acc_ref[idx] + pv

        ahead = slots - 1
        for idx in range(min(ahead, n_maps)):
            scores(idx)
        for idx in range(n_maps):
            if idx + ahead < n_maps:
                scores(idx + ahead)
            accumulate(idx)

    @pl.when((flags & 2) == 0)
    def _():
        step(False)

    @pl.when((flags & 2) != 0)
    def _():
        step(True)

    @pl.when((flags & 4) != 0)
    def _():
        lam = _diff_lambda(lp_ref[...], lam_init)
        for hh in range(A_HEADS):
            outs = []
            for idx in (2 * hh, 2 * hh + 1):
                outs.append(acc_ref[idx, 0:A_V_DIM, :] * (1.0 / acc_ref[idx, A_V_DIM:A_V_DIM + 1, :]))
            o = (outs[0] - lam * outs[1]).T
            o_ref[:, hh * LANES:(hh + 1) * LANES] = _subln(o, g_ref[...], lam_init).astype(BF16)


def _attn_pairs(t, tq, tk):
    qi, kj, fl = [], [], []
    for i in range(t // tq):
        q_lo, q_hi = (i * tq) // CHUNK, (i * tq + tq - 1) // CHUNK
        js = [j for j in range(t // tk) if (j * tk) // CHUNK <= q_hi]
        for j in js:
            partial = (j * tk + tk - 1) // CHUNK > q_lo
            qi.append(i)
            kj.append(j)
            fl.append((1 if j == js[0] else 0) | (2 if partial else 0) | (4 if j == js[-1] else 0))
    return (np.asarray(qi, np.int32), np.asarray(kj, np.int32), np.asarray(fl, np.int32))


def _attn_prompt(q1, q2, kb, vt, lam_p, subln_g, lam_init):
    t = q1.shape[0]
    tq, tk = min(ATTN_Q_TILE, t), min(ATTN_K_TILE, t)
    qi, kj, fl = _attn_pairs(t, tq, tk)
    grid_spec = pltpu.PrefetchScalarGridSpec(
        num_scalar_prefetch=3,
        grid=(len(qi),),
        in_specs=[
            pl.BlockSpec((tq, A_WIDTH), lambda n, qi, kj, fl: (qi[n], 0)),
            pl.BlockSpec((tq, A_WIDTH), lambda n, qi, kj, fl: (qi[n], 0)),
            pl.BlockSpec((tk, A_WIDTH), lambda n, qi, kj, fl: (kj[n], 0)),
            pl.BlockSpec((A_HEADS * V_ROWS, tk), lambda n, qi, kj, fl: (0, kj[n])),
            pl.BlockSpec(lam_p.shape, lambda n, qi, kj, fl: (0, 0)),
            pl.BlockSpec(subln_g.shape, lambda n, qi, kj, fl: (0, 0)),
        ],
        out_specs=pl.BlockSpec((tq, A_WIDTH), lambda n, qi, kj, fl: (qi[n], 0)),
        scratch_shapes=[pltpu.VMEM((2 * A_HEADS, 1, tq), F32),
                        pltpu.VMEM((2 * A_HEADS, V_ROWS, tq), F32),
                        pltpu.VMEM((ATTN_SCORE_SLOTS, tk, tq), F32)],
    )
    return pl.pallas_call(
        functools.partial(_attn_prompt_kernel, lam_init=lam_init),
        grid_spec=grid_spec,
        out_shape=jax.ShapeDtypeStruct((t, A_WIDTH), BF16),
        compiler_params=_params("arbitrary"),
        name="attn_prompt",
    )(jnp.asarray(qi), jnp.asarray(kj), jnp.asarray(fl), q1, q2, kb, vt, lam_p, subln_g)


def _attn_sample_kernel(q1_ref, q2_ref, kn_ref, vn_ref, ckt_ref, cv_ref, lp_ref, g_ref, o_ref, *, lam_init):
    past = cv_ref.shape[0]
    tq = q1_ref.shape[0]
    qpos = past + lax.broadcasted_iota(jnp.int32, (tq, past), 0)
    vis_past = (lax.broadcasted_iota(jnp.int32, (tq, past), 1) // CHUNK) <= (qpos // CHUNK)
    qpos_n = lax.broadcasted_iota(jnp.int32, (tq, tq), 0)
    vis_new = (lax.broadcasted_iota(jnp.int32, (tq, tq), 1) + past) // CHUNK <= (qpos_n + past) // CHUNK
    lam = _diff_lambda(lp_ref[...], lam_init)
    nt = (((1,), (1,)), ((), ()))
    for hh in range(A_HEADS):
        sl = slice(hh * LANES, (hh + 1) * LANES)
        ckt = ckt_ref[2 * hh:2 * hh + 2].reshape(2 * A_HEAD_DIM, past).astype(BF16)
        cv = cv_ref[:, hh, :].astype(BF16)
        kn = kn_ref[:, sl]
        vn = vn_ref[:, sl].astype(BF16)
        outs = []
        for q_ref in (q1_ref, q2_ref):
            q = q_ref[:, sl]
            s_p = jnp.where(vis_past, jnp.dot(q, ckt, preferred_element_type=F32), MASK_VALUE)
            s_n = jnp.where(vis_new, lax.dot_general(q, kn, nt, preferred_element_type=F32), MASK_VALUE)
            m = jnp.maximum(jnp.max(s_p, axis=-1, keepdims=True), jnp.max(s_n, axis=-1, keepdims=True))
            p_p = jnp.exp2(s_p - m)
            p_n = jnp.exp2(s_n - m)
            l = jnp.sum(p_p, axis=-1, keepdims=True) + jnp.sum(p_n, axis=-1, keepdims=True)
            pv = (jnp.dot(p_p.astype(BF16), cv, preferred_element_type=F32)
                  + jnp.dot(p_n.astype(BF16), vn, preferred_element_type=F32))
            outs.append(pv * (1.0 / l))
        o = outs[0] - lam * outs[1]
        o_ref[:, sl] = _subln(o, g_ref[...], lam_init).astype(BF16)


def _attn_sample(q1, q2, kb, vf, cache_kt, cache_v, layer, lam_p, subln_g, lam_init):
    _, nb, nh2, hd, past = cache_kt.shape
    tq = q1.shape[0] // nb
    new = lambda: pl.BlockSpec((tq, A_WIDTH), lambda b: (b, 0))
    return pl.pallas_call(
        functools.partial(_attn_sample_kernel, lam_init=lam_init),
        grid=(nb,),
        in_specs=[new(), new(), new(), new(),
                  pl.BlockSpec((None, None, nh2, hd, past), lambda b: (layer, b, 0, 0, 0)),
                  pl.BlockSpec((None, None, past, A_HEADS, A_V_DIM), lambda b: (layer, b, 0, 0, 0)),
                  pl.BlockSpec(lam_p.shape, lambda b: (0, 0)), pl.BlockSpec(subln_g.shape, lambda b: (0, 0))],
        out_specs=new(),
        out_shape=jax.ShapeDtypeStruct((nb * tq, A_WIDTH), BF16),
        compiler_params=_params("parallel"),
        name="attn_sample",
    )(q1, q2, kb, vf, cache_kt, cache_v, lam_p, subln_g)


def _gla_kernel(q_ref, k_ref, v_ref, la_ref, r_ref, s0_ref, g_ref, o_ref, s_ref, st_ref, *, chunk):
    blk = pl.program_id(1)
    tokens = q_ref.shape[0]

    @pl.when(blk == 0)
    def _():
        st_ref[...] = s0_ref[...].T

    row = lax.broadcasted_iota(jnp.int32, (chunk, chunk), 0)
    col = lax.broadcasted_iota(jnp.int32, (chunk, chunk), 1)
    causal = col <= row
    tri = jnp.where(causal, 1.0, 0.0).astype(BF16)
    lane = lax.broadcasted_iota(jnp.int32, (chunk, B_KEYS), 1)
    nt = (((1,), (1,)), ((), ()))
    tn = (((0,), (0,)), ((), ()))
    for c in range(tokens // chunk):
        r = slice(c * chunk, (c + 1) * chunk)
        la = la_ref[r, :]
        la_hi = la.astype(BF16)
        la_lo = (la - la_hi.astype(F32)).astype(BF16)
        b = (jnp.dot(tri, la_hi, preferred_element_type=F32)
             + jnp.dot(tri, la_lo, preferred_element_type=F32))
        b_last = b[chunk - 1:chunk, :]
        qe = q_ref[r, :] * jnp.exp(b)
        k = k_ref[r, :]
        ke = (k * jnp.exp(-b)).astype(BF16)
        kd = k * jnp.exp(b_last - b)
        st = st_ref[...]
        st_bf = st.astype(BF16)
        st_new = st * jnp.exp(b_last)
        for hh in range(B_HEADS):
            in_head = (lane // B_KEY_DIM) == hh
            vs = slice(hh * B_VAL_DIM, (hh + 1) * B_VAL_DIM)
            v = v_ref[r, vs]
            qe_h = jnp.where(in_head, qe, 0.0).astype(BF16)
            att = lax.dot_general(qe_h, ke, nt, preferred_element_type=F32)
            att = jnp.where(causal, att, 0.0).astype(BF16)
            o = (jnp.dot(att, v, preferred_element_type=F32)
                 + lax.dot_general(qe_h, st_bf, nt, preferred_element_type=F32))
            rg = r_ref[r, vs]
            o_ref[r, vs] = (_rms(o, g_ref[...], SUBLN_EPS) * (rg * jax.nn.sigmoid(rg))).astype(BF16)
            kd_h = jnp.where(in_head, kd, 0.0).astype(BF16)
            st_new = st_new + lax.dot_general(v, kd_h, tn, preferred_element_type=F32)
        st_ref[...] = st_new

    @pl.when(blk == pl.num_programs(1) - 1)
    def _():
        s_ref[...] = st_ref[...].T


def _gla(q, k, v, la, rg, s0, layer, g, chunk):
    nb = s0.shape[1]
    t = q.shape[0] // nb
    tb = min(GLA_BLOCK, t)
    nblk = t // tb
    tok = lambda n: pl.BlockSpec((tb, n), lambda b, i: (b * nblk + i, 0))
    state = pl.BlockSpec((None, B_KEYS, B_VAL_DIM), lambda b, i: (b, 0, 0))
    return pl.pallas_call(
        functools.partial(_gla_kernel, chunk=chunk),
        grid=(nb, nblk),
        in_specs=[tok(B_KEYS), tok(B_KEYS), tok(B_WIDTH), tok(B_KEYS), tok(B_WIDTH),
                  pl.BlockSpec((None, None, B_KEYS, B_VAL_DIM), lambda b, i: (layer, b, 0, 0)),
                  pl.BlockSpec(g.shape, lambda b, i: (0, 0))],
        out_specs=[tok(B_WIDTH), state],
        out_shape=[jax.ShapeDtypeStruct((nb * t, B_WIDTH), BF16),
                   jax.ShapeDtypeStruct((nb, B_KEYS, B_VAL_DIM), F32)],
        scratch_shapes=[pltpu.VMEM((B_VAL_DIM, B_KEYS), F32)],
        compiler_params=_params("parallel", "arbitrary"),
        name="gla",
    )(q, k, v, la, rg, s0, g)


def _merge_kernel(x_ref, oa_ref, ob_ref, gpre_ref, gpost_ref, wgab_ref, woa_ref, wob_ref, wo_ref, o_ref):
    x = x_ref[...]
    d = x.shape[1]
    h = _rms(x, gpre_ref[...], NORM_EPS).astype(BF16)
    gab = jnp.dot(h, wgab_ref[...], preferred_element_type=F32)
    ya = jnp.dot(oa_ref[...], woa_ref[...], preferred_element_type=F32)
    yb = jnp.dot(ob_ref[...], wob_ref[...], preferred_element_type=F32)
    m = jax.nn.sigmoid(gab[:, :d]) * ya + jax.nn.sigmoid(gab[:, d:]) * yb
    y = jnp.dot(m.astype(BF16), wo_ref[...], preferred_element_type=F32)
    o_ref[...] = x + _rms(y, gpost_ref[...], NORM_EPS)


def _merge(x, oa, ob, g_pre, g_post, wgab, woa, wob, wo):
    t, d = x.shape
    tm = min(ROW_TILE, t)
    row = lambda n: pl.BlockSpec((tm, n), lambda i: (i, 0))
    full = lambda a: pl.BlockSpec(a.shape, lambda i: (0, 0))
    return pl.pallas_call(
        _merge_kernel,
        grid=(t // tm,),
        in_specs=[row(d), row(A_WIDTH), row(B_WIDTH), full(g_pre), full(g_post),
                  full(wgab), full(woa), full(wob), full(wo)],
        out_specs=row(d),
        out_shape=jax.ShapeDtypeStruct((t, d), F32),
        compiler_params=_params("parallel"),
        name="mixer_merge",
    )(x, oa, ob, g_pre, g_post, wgab, woa, wob, wo)


def _rope_tables(pos):
    half = A_HEAD_DIM // 2
    inv = jnp.exp(-math.log(ROPE_THETA) * jnp.arange(half, dtype=F32) / half)
    ang = pos.astype(F32)[:, None] * inv[None, :]
    cos, sin = jnp.cos(ang), jnp.sin(ang)
    reps = LANES // A_HEAD_DIM
    return (jnp.tile(jnp.concatenate([cos, cos], axis=-1), (1, reps)),
            jnp.tile(jnp.concatenate([-sin, sin], axis=-1), (1, reps)))


def kernel(x_prompt, x_sample, cache_k, cache_v, state_gla, norm_g, ffn_w_gate, ffn_w_up, ffn_w_down, w_in,
           w_gate_up, b_gate, lambda_p, subln_g, gla_norm_g, w_out_a, w_out_b, w_out):
    depth = w_in.shape[0]
    bp, tp, d = x_prompt.shape
    bs, ts, _ = x_sample.shape
    past = cache_k.shape[2]
    assert bp == 1 and tp % CHUNK == 0

    n_qk = 2 * A_HEADS * A_HEAD_DIM
    sizes = (n_qk, n_qk, A_WIDTH, B_KEYS, B_KEYS, B_WIDTH, B_WIDTH, GATE_RANK, d, d)
    offs = np.concatenate([[0], np.cumsum(sizes)])
    assert w_in.shape[2] == offs[-1]
    code_lo, code_hi = int(offs[7]), int(offs[8])

    cos_p, sin_p = _rope_tables(jnp.arange(tp))
    cos_s, sin_s = _rope_tables(jnp.tile(past + jnp.arange(ts), bs))

    cache_kt = jnp.transpose(cache_k, (0, 1, 3, 4, 2))
    s0_sample = state_gla.reshape(depth, bs, B_KEYS, B_VAL_DIM)
    s0_prompt = jnp.zeros((1, bp, B_KEYS, B_VAL_DIM), F32)

    xp = x_prompt.reshape(tp, d)
    xs = x_sample.reshape(bs * ts, d)
    outs = [[] for _ in range(6)]
    for l in range(depth):
        lam_init = 0.8 - 0.6 * math.exp(-0.3 * l)
        norm = [norm_g[l, i][None, :] for i in range(6)]
        wg = ffn_w_gate[l].astype(BF16)
        wu = ffn_w_up[l].astype(BF16)
        wd = ffn_w_down[l].astype(BF16)
        w_l = w_in[l]
        w_code = jnp.pad(w_l[:, code_lo:code_hi], ((0, 0), (0, LANES - GATE_RANK)))
        w_proj = jnp.concatenate([w_l[:, :code_lo], w_code], axis=1).astype(BF16)
        w_gab = w_l[:, code_hi:].astype(BF16)
        w_gu = jnp.pad(w_gate_up[l], ((0, LANES - GATE_RANK), (0, 0))).astype(BF16)
        bg = b_gate[l][None, :]
        woa = w_out_a[l].astype(BF16)
        wob = w_out_b[l].astype(BF16)
        wo = w_out[l].astype(BF16)
        sg = subln_g[l][None, :]
        gg = gla_norm_g[l][None, :]
        lam_p = lambda_p[l]

        def layer(x, cos, sin, chunk, s0, s0_layer, is_prompt):
            x = _ffn(x, norm[0], norm[1], wg[0], wu[0], wd[0])
            kf, vf, q1, q2, kb, vt, qg, kg, vg, rg, la = _proj(x, norm[2], w_proj, w_gu, bg, cos, sin, is_prompt)
            if is_prompt:
                oa = _attn_prompt(q1, q2, kb, vt, lam_p, sg, lam_init)
            else:
                oa = _attn_sample(q1, q2, kb, vf, cache_kt, cache_v, l, lam_p, sg, lam_init)
            ob, s_new = _gla(qg, kg, vg, la, rg, s0, s0_layer, gg, chunk)
            x = _merge(x, oa, ob, norm[2], norm[3], w_gab, woa, wob, wo)
            x = _ffn(x, norm[4], norm[5], wg[1], wu[1], wd[1])
            return x, kf, vf, s_new

        xp, kp, vp, sp = layer(xp, cos_p, sin_p, CHUNK, s0_prompt, 0, True)
        xs, kn, vn, sn = layer(xs, cos_s, sin_s, ts, s0_sample, l, False)
        for lst, val in zip(outs, (kp, vp, sp, kn, vn, sn)):
            lst.append(val)

    k_prompt = jnp.transpose(jnp.stack(outs[0]).reshape(depth, bp, 2 * A_HEADS, A_HEAD_DIM, tp), (0, 1, 4, 2, 3))
    v_prompt = jnp.stack(outs[1]).reshape(depth, bp, tp, A_HEADS, A_V_DIM)
    s_prompt = jnp.stack(outs[2]).reshape(depth, bp, B_HEADS, B_KEY_DIM, B_VAL_DIM)
    k_sample = jnp.stack(outs[3]).reshape(depth, bs, ts, 2 * A_HEADS, A_HEAD_DIM)
    v_sample = jnp.stack(outs[4]).reshape(depth, bs, ts, A_HEADS, A_V_DIM)
    s_sample = jnp.stack(outs[5]).reshape(depth, bs, B_HEADS, B_KEY_DIM, B_VAL_DIM)
    return (xp.reshape(bp, tp, d), xs.reshape(bs, ts, d), k_prompt, v_prompt, s_prompt,
            k_sample, v_sample, s_sample)
```

```python
import functools
import math

import jax
import jax.numpy as jnp
import numpy as np
from jax import lax
from jax.experimental import pallas as pl
from jax.experimental.pallas import tpu as pltpu

F32 = jnp.float32
BF16 = jnp.bfloat16

CHUNK = 64
A_HEADS = 4
A_HEAD_DIM = 64
A_V_DIM = 2 * A_HEAD_DIM
A_WIDTH = A_HEADS * A_V_DIM
B_HEADS = 4
B_KEY_DIM = 64
B_VAL_DIM = 128
B_KEYS = B_HEADS * B_KEY_DIM
B_WIDTH = B_HEADS * B_VAL_DIM
GATE_RANK = 16
GATE_TAU = 16.0
ROPE_THETA = 10000.0
NORM_EPS = 1e-6
SUBLN_EPS = 1e-5
LANES = 128
MASK_VALUE = -1e30

ROW_TILE = 512
ATTN_Q_TILE = 512
ATTN_K_TILE = 1024
ATTN_SCORE_SLOTS = 3
V_ROWS = A_V_DIM + 16
LOG2E = math.log2(math.e)
GLA_BLOCK = 512
VMEM_LIMIT = 56 * 1024 * 1024


def _rms(x, g, eps):
    return x * lax.rsqrt(jnp.mean(x * x, axis=-1, keepdims=True) + eps) * g


def _params(*sem):
    return pltpu.CompilerParams(dimension_semantics=sem, vmem_limit_bytes=VMEM_LIMIT)


def _ffn_kernel(x_ref, gpre_ref, gpost_ref, wg_ref, wu_ref, wd_ref, o_ref):
    x = x_ref[...]
    h = _rms(x, gpre_ref[...], NORM_EPS).astype(BF16)
    g = jnp.dot(h, wg_ref[...], preferred_element_type=F32)
    u = jnp.dot(h, wu_ref[...], preferred_element_type=F32)
    a = (g * jax.nn.sigmoid(g) * u).astype(BF16)
    y = jnp.dot(a, wd_ref[...], preferred_element_type=F32)
    o_ref[...] = x + 0.5 * _rms(y, gpost_ref[...], NORM_EPS)


def _resident(a):
    return pl.BlockSpec(a.shape, lambda i: (0,) * a.ndim, pipeline_mode=pl.Buffered(1))


def _ffn(x, g_pre, g_post, wg, wu, wd):
    t, d = x.shape
    tm = min(ROW_TILE, t)
    row = pl.BlockSpec((tm, d), lambda i: (i, 0))
    return pl.pallas_call(
        _ffn_kernel,
        grid=(t // tm,),
        in_specs=[row, _resident(g_pre), _resident(g_post), _resident(wg), _resident(wu), _resident(wd)],
        out_specs=row,
        out_shape=jax.ShapeDtypeStruct((t, d), F32),
        compiler_params=_params("parallel"),
        name="ffn",
    )(x, g_pre, g_post, wg, wu, wd)


def _rope(z, cos, sin_signed, first_half):
    rot = jnp.where(first_half, pltpu.roll(z, LANES - A_HEAD_DIM // 2, 1), pltpu.roll(z, A_HEAD_DIM // 2, 1))
    return z * cos + rot * sin_signed


def _proj_kernel(x_ref, g_ref, w_ref, wgu_ref, bg_ref, cos_ref, sin_ref,
                 kf_ref, vf_ref, q1_ref, q2_ref, kb_ref, vt_ref,
                 qg_ref, kg_ref, vg_ref, rg_ref, la_ref, *, k_transposed):
    h = _rms(x_ref[...], g_ref[...], NORM_EPS).astype(BF16)
    z = jnp.dot(h, w_ref[...], preferred_element_type=F32)
    cos = cos_ref[...]
    sin = sin_ref[...]
    lane = lax.broadcasted_iota(jnp.int32, cos.shape, 1)
    first_half = (lane % A_HEAD_DIM) < (A_HEAD_DIM // 2)
    even_head = lane < A_HEAD_DIM
    scale = A_HEAD_DIM ** -0.5 * LOG2E
    o = 0
    for hh in range(A_HEADS):
        sl = slice(hh * LANES, (hh + 1) * LANES)
        q = _rope(z[:, o + hh * LANES:o + (hh + 1) * LANES], cos, sin, first_half) * scale
        q1_ref[:, sl] = jnp.where(even_head, q, 0.0).astype(BF16)
        q2_ref[:, sl] = jnp.where(even_head, 0.0, q).astype(BF16)
    o += 2 * A_HEADS * A_HEAD_DIM
    for hh in range(A_HEADS):
        sl = slice(hh * LANES, (hh + 1) * LANES)
        k = _rope(z[:, o + hh * LANES:o + (hh + 1) * LANES], cos, sin, first_half)
        if k_transposed:
            kf_ref[sl, :] = k.T
        else:
            kf_ref[:, sl] = k
        kb_ref[:, sl] = k.astype(BF16)
    o += 2 * A_HEADS * A_HEAD_DIM
    va = z[:, o:o + A_WIDTH]
    for hh in range(A_HEADS):
        vf_ref[pl.ds(hh, va.shape[0], stride=A_HEADS), :] = va[:, hh * A_V_DIM:(hh + 1) * A_V_DIM]
        vt_ref[hh * V_ROWS:hh * V_ROWS + A_V_DIM, :] = va[:, hh * A_V_DIM:(hh + 1) * A_V_DIM].T.astype(BF16)
        vt_ref[hh * V_ROWS + A_V_DIM:(hh + 1) * V_ROWS, :] = jnp.ones((V_ROWS - A_V_DIM, va.shape[0]), BF16)
    o += A_WIDTH
    qg_ref[...] = z[:, o:o + B_KEYS] * (B_KEY_DIM ** -0.5)
    o += B_KEYS
    kg_ref[...] = z[:, o:o + B_KEYS]
    o += B_KEYS
    vg_ref[...] = z[:, o:o + B_WIDTH].astype(BF16)
    o += B_WIDTH
    rg_ref[...] = z[:, o:o + B_WIDTH]
    o += B_WIDTH
    code = z[:, o:o + LANES].astype(BF16)
    gate = jnp.dot(code, wgu_ref[...], preferred_element_type=F32) + bg_ref[...]
    la_ref[...] = (jnp.minimum(gate, 0.0) - jnp.log1p(jnp.exp(-jnp.abs(gate)))) * (1.0 / GATE_TAU)


def _proj(x, g, w, wgu, bg, cos, sin, k_transposed):
    t, d = x.shape
    tm = min(ROW_TILE, t)
    row = lambda n: pl.BlockSpec((tm, n), lambda i: (i, 0))
    col = lambda n: pl.BlockSpec((n, tm), lambda i: (0, i))
    sds = jax.ShapeDtypeStruct
    n_qk = 2 * A_HEADS * A_HEAD_DIM
    return pl.pallas_call(
        functools.partial(_proj_kernel, k_transposed=k_transposed),
        grid=(t // tm,),
        in_specs=[row(d), _resident(g), _resident(w), _resident(wgu), _resident(bg), row(LANES), row(LANES)],
        out_specs=[col(n_qk) if k_transposed else row(n_qk),
                   pl.BlockSpec((tm * A_HEADS, A_V_DIM), lambda i: (i, 0)),
                   row(A_WIDTH), row(A_WIDTH), row(A_WIDTH), col(A_HEADS * V_ROWS),
                   row(B_KEYS), row(B_KEYS), row(B_WIDTH), row(B_WIDTH), row(B_KEYS)],
        out_shape=[sds((n_qk, t) if k_transposed else (t, n_qk), F32), sds((t * A_HEADS, A_V_DIM), F32),
                   sds((t, A_WIDTH), BF16), sds((t, A_WIDTH), BF16), sds((t, A_WIDTH), BF16),
                   sds((A_HEADS * V_ROWS, t), BF16),
                   sds((t, B_KEYS), F32), sds((t, B_KEYS), F32), sds((t, B_WIDTH), BF16),
                   sds((t, B_WIDTH), F32), sds((t, B_KEYS), F32)],
        compiler_params=_params("parallel"),
        name="mixer_proj",
    )(x, g, w, wgu, bg, cos, sin)


def _diff_lambda(lp, lam_init):
    a = jnp.sum(lp[0:1] * lp[1:2], axis=-1, keepdims=True)
    b = jnp.sum(lp[2:3] * lp[3:4], axis=-1, keepdims=True)
    return jnp.exp(a) - jnp.exp(b) + lam_init


def _subln(o, g, lam_init):
    return _rms(o, g, SUBLN_EPS) * (1.0 - lam_init)


def _attn_prompt_kernel(qi_ref, kj_ref, fl_ref, q1_ref, q2_ref, k_ref, vt_ref, lp_ref, g_ref,
                        o_ref, m_ref, acc_ref, st_ref, *, lam_init):
    n = pl.program_id(0)
    tk = k_ref.shape[0]
    flags = fl_ref[n]
    n_maps = 2 * A_HEADS
    slots = st_ref.shape[0]
    nt = (((1,), (1,)), ((), ()))

    @pl.when((flags & 1) != 0)
    def _():
        m_ref[...] = jnp.full(m_ref.shape, MASK_VALUE, F32)
        acc_ref[...] = jnp.zeros(acc_ref.shape, F32)

    def step(masked):
        if masked:
            shape = st_ref.shape[1:]
            kpos = kj_ref[n] * tk + lax.broadcasted_iota(jnp.int32, shape, 0)
            qpos = qi_ref[n] * shape[1] + lax.broadcasted_iota(jnp.int32, shape, 1)
            visible = (kpos // CHUNK) <= (qpos // CHUNK)

        def scores(idx):
            hh, c = divmod(idx, 2)
            sl = slice(hh * LANES, (hh + 1) * LANES)
            st = lax.dot_general(k_ref[:, sl], (q1_ref, q2_ref)[c][:, sl], nt, preferred_element_type=F32)
            if masked:
                st = jnp.where(visible, st, MASK_VALUE)
            st_ref[idx % slots] = st

        def accumulate(idx):
            hh = idx // 2
            st = st_ref[idx % slots]
            m_old = m_ref[idx]
            m_new = jnp.maximum(m_old, jnp.max(st, axis=0, keepdims=True))
            m_ref[idx] = m_new
            p = jnp.exp2(st - m_new).astype(BF16)
            pv = jnp.dot(vt_ref[hh * V_ROWS:(hh + 1) * V_ROWS, :], p, preferred_element_type=F32)
            acc_ref[idx] = jnp.exp2(m_old - m_new) * acc_ref[idx] + pv

        ahead = slots - 1
        for idx in range(min(ahead, n_maps)):
            scores(idx)
        for idx in range(n_maps):
            if idx + ahead < n_maps:
                scores(idx + ahead)
            accumulate(idx)

    @pl.when((flags & 2) == 0)
    def _():
        step(False)

    @pl.when((flags & 2) != 0)
    def _():
        step(True)

    @pl.when((flags & 4) != 0)
    def _():
        lam = _diff_lambda(lp_ref[...], lam_init)
        for hh in range(A_HEADS):
            outs = []
            for idx in (2 * hh, 2 * hh + 1):
                outs.append(acc_ref[idx, 0:A_V_DIM, :] * (1.0 / acc_ref[idx, A_V_DIM:A_V_DIM + 1, :]))
            o = (outs[0] - lam * outs[1]).T
            o_ref[:, hh * LANES:(hh + 1) * LANES] = _subln(o, g_ref[...], lam_init).astype(BF16)


def _attn_pairs(t, tq, tk):
    qi, kj, fl = [], [], []
    for i in range(t // tq):
        q_lo, q_hi = (i * tq) // CHUNK, (i * tq + tq - 1) // CHUNK
        js = [j for j in range(t // tk) if (j * tk) // CHUNK <= q_hi]
        for j in js:
            partial = (j * tk + tk - 1) // CHUNK > q_lo
            qi.append(i)
            kj.append(j)
            fl.append((1 if j == js[0] else 0) | (2 if partial else 0) | (4 if j == js[-1] else 0))
    return (np.asarray(qi, np.int32), np.asarray(kj, np.int32), np.asarray(fl, np.int32))


def _attn_prompt(q1, q2, kb, vt, lam_p, subln_g, lam_init):
    t = q1.shape[0]
    tq, tk = min(ATTN_Q_TILE, t), min(ATTN_K_TILE, t)
    qi, kj, fl = _attn_pairs(t, tq, tk)
    grid_spec = pltpu.PrefetchScalarGridSpec(
        num_scalar_prefetch=3,
        grid=(len(qi),),
        in_specs=[
            pl.BlockSpec((tq, A_WIDTH), lambda n, qi, kj, fl: (qi[n], 0)),
            pl.BlockSpec((tq, A_WIDTH), lambda n, qi, kj, fl: (qi[n], 0)),
            pl.BlockSpec((tk, A_WIDTH), lambda n, qi, kj, fl: (kj[n], 0)),
            pl.BlockSpec((A_HEADS * V_ROWS, tk), lambda n, qi, kj, fl: (0, kj[n])),
            pl.BlockSpec(lam_p.shape, lambda n, qi, kj, fl: (0, 0)),
            pl.BlockSpec(subln_g.shape, lambda n, qi, kj, fl: (0, 0)),
        ],
        out_specs=pl.BlockSpec((tq, A_WIDTH), lambda n, qi, kj, fl: (qi[n], 0)),
        scratch_shapes=[pltpu.VMEM((2 * A_HEADS, 1, tq), F32),
                        pltpu.VMEM((2 * A_HEADS, V_ROWS, tq), F32),
                        pltpu.VMEM((ATTN_SCORE_SLOTS, tk, tq), F32)],
    )
    return pl.pallas_call(
        functools.partial(_attn_prompt_kernel, lam_init=lam_init),
        grid_spec=grid_spec,
        out_shape=jax.ShapeDtypeStruct((t, A_WIDTH), BF16),
        compiler_params=_params("arbitrary"),
        name="attn_prompt",
    )(jnp.asarray(qi), jnp.asarray(kj), jnp.asarray(fl), q1, q2, kb, vt, lam_p, subln_g)


def _attn_sample_kernel(q1_ref, q2_ref, kn_ref, vn_ref, ckt_ref, cv_ref, lp_ref, g_ref, o_ref, *, lam_init):
    past = cv_ref.shape[0] // A_HEADS
    tq = q1_ref.shape[0]
    qpos = past + lax.broadcasted_iota(jnp.int32, (tq, past), 0)
    vis_past = (lax.broadcasted_iota(jnp.int32, (tq, past), 1) // CHUNK) <= (qpos // CHUNK)
    qpos_n = lax.broadcasted_iota(jnp.int32, (tq, tq), 0)
    vis_new = (lax.broadcasted_iota(jnp.int32, (tq, tq), 1) + past) // CHUNK <= (qpos_n + past) // CHUNK
    lam = _diff_lambda(lp_ref[...], lam_init)
    nt = (((1,), (1,)), ((), ()))
    for hh in range(A_HEADS):
        sl = slice(hh * LANES, (hh + 1) * LANES)
        ckt = ckt_ref[2 * hh:2 * hh + 2].reshape(2 * A_HEAD_DIM, past).astype(BF16)
        cv = cv_ref[pl.ds(hh, past, stride=A_HEADS), :].astype(BF16)
        kn = kn_ref[:, sl]
        vn = vn_ref[pl.ds(hh, tq, stride=A_HEADS), :].astype(BF16)
        outs = []
        for q_ref in (q1_ref, q2_ref):
            q = q_ref[:, sl]
            s_p = jnp.where(vis_past, jnp.dot(q, ckt, preferred_element_type=F32), MASK_VALUE)
            s_n = jnp.where(vis_new, lax.dot_general(q, kn, nt, preferred_element_type=F32), MASK_VALUE)
            m = jnp.maximum(jnp.max(s_p, axis=-1, keepdims=True), jnp.max(s_n, axis=-1, keepdims=True))
            p_p = jnp.exp2(s_p - m)
            p_n = jnp.exp2(s_n - m)
            l = jnp.sum(p_p, axis=-1, keepdims=True) + jnp.sum(p_n, axis=-1, keepdims=True)
            pv = (jnp.dot(p_p.astype(BF16), cv, preferred_element_type=F32)
                  + jnp.dot(p_n.astype(BF16), vn, preferred_element_type=F32))
            outs.append(pv * (1.0 / l))
        o = outs[0] - lam * outs[1]
        o_ref[:, sl] = _subln(o, g_ref[...], lam_init).astype(BF16)


def _attn_sample(q1, q2, kb, vf, cache_kt, cache_v, layer, lam_p, subln_g, lam_init):
    _, nb, nh2, hd, past = cache_kt.shape
    tq = q1.shape[0] // nb
    new = lambda: pl.BlockSpec((tq, A_WIDTH), lambda b: (b, 0))
    return pl.pallas_call(
        functools.partial(_attn_sample_kernel, lam_init=lam_init),
        grid=(nb,),
        in_specs=[new(), new(), new(), pl.BlockSpec((tq * A_HEADS, A_V_DIM), lambda b: (b, 0)),
                  pl.BlockSpec((None, None, nh2, hd, past), lambda b: (layer, b, 0, 0, 0)),
                  pl.BlockSpec((None, None, past * A_HEADS, A_V_DIM), lambda b: (layer, b, 0, 0)),
                  pl.BlockSpec(lam_p.shape, lambda b: (0, 0)), pl.BlockSpec(subln_g.shape, lambda b: (0, 0))],
        out_specs=new(),
        out_shape=jax.ShapeDtypeStruct((nb * tq, A_WIDTH), BF16),
        compiler_params=_params("parallel"),
        name="attn_sample",
    )(q1, q2, kb, vf, cache_kt, cache_v, lam_p, subln_g)


def _gla_kernel(q_ref, k_ref, v_ref, la_ref, r_ref, s0_ref, g_ref, o_ref, s_ref, st_ref, *, chunk):
    blk = pl.program_id(1)
    tokens = q_ref.shape[0]

    @pl.when(blk == 0)
    def _():
        st_ref[...] = s0_ref[...].T

    row = lax.broadcasted_iota(jnp.int32, (tokens, tokens), 0)
    col = lax.broadcasted_iota(jnp.int32, (tokens, tokens), 1)
    same_chunk = (row // chunk) == (col // chunk)
    causal = jnp.where(same_chunk, col, tokens) <= row
    tri = jnp.where(causal, 1.0, 0.0).astype(BF16)
    ones = jnp.where(same_chunk, 1.0, 0.0).astype(BF16)
    lane = lax.broadcasted_iota(jnp.int32, (tokens, B_KEYS), 1)
    nt = (((1,), (1,)), ((), ()))
    tn = (((0,), (0,)), ((), ()))

    la = la_ref[...]
    la_hi = la.astype(BF16)
    la_lo = (la - la_hi.astype(F32)).astype(BF16)
    b = (jnp.dot(tri, la_hi, preferred_element_type=F32)
         + jnp.dot(tri, la_lo, preferred_element_type=F32))
    total = (jnp.dot(ones, la_hi, preferred_element_type=F32)
             + jnp.dot(ones, la_lo, preferred_element_type=F32))
    qe = q_ref[...] * jnp.exp(b)
    k = k_ref[...]
    ke = (k * jnp.exp(-b)).astype(BF16)
    kd = k * jnp.exp(total - b)
    qe_h, kd_h, o_intra = [], [], []
    for hh in range(B_HEADS):
        in_head = (lane // B_KEY_DIM) == hh
        qe_h.append(jnp.where(in_head, qe, 0.0).astype(BF16))
        kd_h.append(jnp.where(in_head, kd, 0.0).astype(BF16))
        att = lax.dot_general(qe_h[hh], ke, nt, preferred_element_type=F32)
        att = jnp.where(causal, att, 0.0).astype(BF16)
        o_intra.append(jnp.dot(att, v_ref[:, hh * B_VAL_DIM:(hh + 1) * B_VAL_DIM], preferred_element_type=F32))

    st = st_ref[...]
    for c in range(tokens // chunk):
        r = slice(c * chunk, (c + 1) * chunk)
        q_stack = jnp.concatenate([qe_h[hh][r] for hh in range(B_HEADS)], axis=0)
        kd_stack = jnp.concatenate([kd_h[hh][r] for hh in range(B_HEADS)], axis=0)
        v_stack = jnp.concatenate([v_ref[r, hh * B_VAL_DIM:(hh + 1) * B_VAL_DIM] for hh in range(B_HEADS)], axis=0)
        o_inter = lax.dot_general(q_stack, st.astype(BF16), nt, preferred_element_type=F32)
        for hh in range(B_HEADS):
            vs = slice(hh * B_VAL_DIM, (hh + 1) * B_VAL_DIM)
            o = o_intra[hh][r] + o_inter[hh * chunk:(hh + 1) * chunk]
            rg = r_ref[r, vs]
            o_ref[r, vs] = (_rms(o, g_ref[...], SUBLN_EPS) * (rg * jax.nn.sigmoid(rg))).astype(BF16)
        st = (st * jnp.exp(total[c * chunk:c * chunk + 1, :])
              + lax.dot_general(v_stack, kd_stack, tn, preferred_element_type=F32))
    st_ref[...] = st

    @pl.when(blk == pl.num_programs(1) - 1)
    def _():
        s_ref[...] = st_ref[...].T


def _gla(q, k, v, la, rg, s0, layer, g, chunk):
    nb = s0.shape[1]
    t = q.shape[0] // nb
    tb = min(GLA_BLOCK, t)
    nblk = t // tb
    tok = lambda n: pl.BlockSpec((tb, n), lambda b, i: (b * nblk + i, 0))
    state = pl.BlockSpec((None, B_KEYS, B_VAL_DIM), lambda b, i: (b, 0, 0))
    return pl.pallas_call(
        functools.partial(_gla_kernel, chunk=chunk),
        grid=(nb, nblk),
        in_specs=[tok(B_KEYS), tok(B_KEYS), tok(B_WIDTH), tok(B_KEYS), tok(B_WIDTH),
                  pl.BlockSpec((None, None, B_KEYS, B_VAL_DIM), lambda b, i: (layer, b, 0, 0)),
                  pl.BlockSpec(g.shape, lambda b, i: (0, 0))],
        out_specs=[tok(B_WIDTH), state],
        out_shape=[jax.ShapeDtypeStruct((nb * t, B_WIDTH), BF16),
                   jax.ShapeDtypeStruct((nb, B_KEYS, B_VAL_DIM), F32)],
        scratch_shapes=[pltpu.VMEM((B_VAL_DIM, B_KEYS), F32)],
        compiler_params=_params("parallel", "arbitrary"),
        name="gla",
    )(q, k, v, la, rg, s0, g)


def _merge_kernel(x_ref, oa_ref, ob_ref, gpre_ref, gpost_ref, wgab_ref, woa_ref, wob_ref, wo_ref, o_ref):
    x = x_ref[...]
    d = x.shape[1]
    h = _rms(x, gpre_ref[...], NORM_EPS).astype(BF16)
    gab = jnp.dot(h, wgab_ref[...], preferred_element_type=F32)
    ya = jnp.dot(oa_ref[...], woa_ref[...], preferred_element_type=F32)
    yb = jnp.dot(ob_ref[...], wob_ref[...], preferred_element_type=F32)
    m = jax.nn.sigmoid(gab[:, :d]) * ya + jax.nn.sigmoid(gab[:, d:]) * yb
    y = jnp.dot(m.astype(BF16), wo_ref[...], preferred_element_type=F32)
    o_ref[...] = x + _rms(y, gpost_ref[...], NORM_EPS)


def _merge(x, oa, ob, g_pre, g_post, wgab, woa, wob, wo):
    t, d = x.shape
    tm = min(ROW_TILE, t)
    row = lambda n: pl.BlockSpec((tm, n), lambda i: (i, 0))
    full = lambda a: pl.BlockSpec(a.shape, lambda i: (0, 0))
    return pl.pallas_call(
        _merge_kernel,
        grid=(t // tm,),
        in_specs=[row(d), row(A_WIDTH), row(B_WIDTH), full(g_pre), full(g_post),
                  full(wgab), full(woa), full(wob), full(wo)],
        out_specs=row(d),
        out_shape=jax.ShapeDtypeStruct((t, d), F32),
        compiler_params=_params("parallel"),
        name="mixer_merge",
    )(x, oa, ob, g_pre, g_post, wgab, woa, wob, wo)


def _rope_tables(pos):
    half = A_HEAD_DIM // 2
    inv = jnp.exp(-math.log(ROPE_THETA) * jnp.arange(half, dtype=F32) / half)
    ang = pos.astype(F32)[:, None] * inv[None, :]
    cos, sin = jnp.cos(ang), jnp.sin(ang)
    reps = LANES // A_HEAD_DIM
    return (jnp.tile(jnp.concatenate([cos, cos], axis=-1), (1, reps)),
            jnp.tile(jnp.concatenate([-sin, sin], axis=-1), (1, reps)))


def kernel(x_prompt, x_sample, cache_k, cache_v, state_gla, norm_g, ffn_w_gate, ffn_w_up, ffn_w_down, w_in,
           w_gate_up, b_gate, lambda_p, subln_g, gla_norm_g, w_out_a, w_out_b, w_out):
    depth = w_in.shape[0]
    bp, tp, d = x_prompt.shape
    bs, ts, _ = x_sample.shape
    past = cache_k.shape[2]
    assert bp == 1 and tp % CHUNK == 0

    n_qk = 2 * A_HEADS * A_HEAD_DIM
    sizes = (n_qk, n_qk, A_WIDTH, B_KEYS, B_KEYS, B_WIDTH, B_WIDTH, GATE_RANK, d, d)
    offs = np.concatenate([[0], np.cumsum(sizes)])
    assert w_in.shape[2] == offs[-1]
    code_lo, code_hi = int(offs[7]), int(offs[8])

    cos_p, sin_p = _rope_tables(jnp.arange(tp))
    cos_s, sin_s = _rope_tables(jnp.tile(past + jnp.arange(ts), bs))

    cache_kt = jnp.transpose(cache_k, (0, 1, 3, 4, 2))
    cache_vr = cache_v.reshape(depth, bs, past * A_HEADS, A_V_DIM)
    s0_sample = state_gla.reshape(depth, bs, B_KEYS, B_VAL_DIM)
    s0_prompt = jnp.zeros((1, bp, B_KEYS, B_VAL_DIM), F32)

    xp = x_prompt.reshape(tp, d)
    xs = x_sample.reshape(bs * ts, d)
    outs = [[] for _ in range(6)]
    for l in range(depth):
        lam_init = 0.8 - 0.6 * math.exp(-0.3 * l)
        norm = [norm_g[l, i][None, :] for i in range(6)]
        wg = ffn_w_gate[l].astype(BF16)
        wu = ffn_w_up[l].astype(BF16)
        wd = ffn_w_down[l].astype(BF16)
        w_l = w_in[l]
        w_code = jnp.pad(w_l[:, code_lo:code_hi], ((0, 0), (0, LANES - GATE_RANK)))
        w_proj = jnp.concatenate([w_l[:, :code_lo], w_code], axis=1).astype(BF16)
        w_gab = w_l[:, code_hi:].astype(BF16)
        w_gu = jnp.pad(w_gate_up[l], ((0, LANES - GATE_RANK), (0, 0))).astype(BF16)
        bg = b_gate[l][None, :]
        woa = w_out_a[l].astype(BF16)
        wob = w_out_b[l].astype(BF16)
        wo = w_out[l].astype(BF16)
        sg = subln_g[l][None, :]
        gg = gla_norm_g[l][None, :]
        lam_p = lambda_p[l]

        def layer(x, cos, sin, chunk, s0, s0_layer, is_prompt):
            x = _ffn(x, norm[0], norm[1], wg[0], wu[0], wd[0])
            kf, vf, q1, q2, kb, vt, qg, kg, vg, rg, la = _proj(x, norm[2], w_proj, w_gu, bg, cos, sin, is_prompt)
            if is_prompt:
                oa = _attn_prompt(q1, q2, kb, vt, lam_p, sg, lam_init)
            else:
                oa = _attn_sample(q1, q2, kb, vf, cache_kt, cache_vr, l, lam_p, sg, lam_init)
            ob, s_new = _gla(qg, kg, vg, la, rg, s0, s0_layer, gg, chunk)
            x = _merge(x, oa, ob, norm[2], norm[3], w_gab, woa, wob, wo)
            x = _ffn(x, norm[4], norm[5], wg[1], wu[1], wd[1])
            return x, kf, vf, s_new

        xp, kp, vp, sp = layer(xp, cos_p, sin_p, CHUNK, s0_prompt, 0, True)
        xs, kn, vn, sn = layer(xs, cos_s, sin_s, ts, s0_sample, l, False)
        for lst, val in zip(outs, (kp, vp, sp, kn, vn, sn)):
            lst.append(val)

    k_prompt = jnp.transpose(jnp.stack(outs[0]).reshape(depth, bp, 2 * A_HEADS, A_HEAD_DIM, tp), (0, 1, 4, 2, 3))
    v_prompt = jnp.stack(outs[1]).reshape(depth, bp, tp, A_HEADS, A_V_DIM)
    s_prompt = jnp.stack(outs[2]).reshape(depth, bp, B_HEADS, B_KEY_DIM, B_VAL_DIM)
    k_sample = jnp.stack(outs[3]).reshape(depth, bs, ts, 2 * A_HEADS, A_HEAD_DIM)
    v_sample = jnp.stack(outs[4]).reshape(depth, bs, ts, A_HEADS, A_V_DIM)
    s_sample = jnp.stack(outs[5]).reshape(depth, bs, B_HEADS, B_KEY_DIM, B_VAL_DIM)
    return (xp.reshape(bp, tp, d), xs.reshape(bs, ts, d), k_prompt, v_prompt, s_prompt,
            k_sample, v_sample, s_sample)
```

```python
import functools
import math

import jax
import jax.numpy as jnp
import numpy as np
from jax import lax
from jax.experimental import pallas as pl
from jax.experimental.pallas import tpu as pltpu

F32 = jnp.float32
BF16 = jnp.bfloat16

CHUNK = 64
A_HEADS = 4
A_HEAD_DIM = 64
A_V_DIM = 2 * A_HEAD_DIM
A_WIDTH = A_HEADS * A_V_DIM
B_HEADS = 4
B_KEY_DIM = 64
B_VAL_DIM = 128
B_KEYS = B_HEADS * B_KEY_DIM
B_WIDTH = B_HEADS * B_VAL_DIM
GATE_RANK = 16
GATE_TAU = 16.0
ROPE_THETA = 10000.0
NORM_EPS = 1e-6
SUBLN_EPS = 1e-5
LANES = 128
MASK_VALUE = -1e30

ROW_TILE = 512
ATTN_Q_TILE = 512
ATTN_K_TILE = 1024
ATTN_SCORE_SLOTS = 3
V_ROWS = A_V_DIM + 16
LOG2E = math.log2(math.e)
GLA_BLOCK = 512
VMEM_LIMIT = 56 * 1024 * 1024


def _rms(x, g, eps):
    return x * lax.rsqrt(jnp.mean(x * x, axis=-1, keepdims=True) + eps) * g


def _params(*sem):
    return pltpu.CompilerParams(dimension_semantics=sem, vmem_limit_bytes=VMEM_LIMIT)


def _ffn_kernel(x_ref, gpre_ref, gpost_ref, wg_ref, wu_ref, wd_ref, o_ref):
    x = x_ref[...]
    h = _rms(x, gpre_ref[...], NORM_EPS).astype(BF16)
    g = jnp.dot(h, wg_ref[...], preferred_element_type=F32)
    u = jnp.dot(h, wu_ref[...], preferred_element_type=F32)
    a = (g * jax.nn.sigmoid(g) * u).astype(BF16)
    y = jnp.dot(a, wd_ref[...], preferred_element_type=F32)
    o_ref[...] = x + 0.5 * _rms(y, gpost_ref[...], NORM_EPS)


def _pick(a, *lead):
    rest = a.shape[len(lead):]
    index = tuple(lead) + (0,) * len(rest)
    return pl.BlockSpec((None,) * len(lead) + rest, lambda *_: index, pipeline_mode=pl.Buffered(1))


def _ffn(x, norm, wg, wu, wd, layer, which):
    t, d = x.shape
    tm = min(ROW_TILE, t)
    row = pl.BlockSpec((tm, d), lambda i: (i, 0))
    return pl.pallas_call(
        _ffn_kernel,
        grid=(t // tm,),
        in_specs=[row, _pick(norm, layer, 4 * which), _pick(norm, layer, 4 * which + 1),
                  _pick(wg, layer, which), _pick(wu, layer, which), _pick(wd, layer, which)],
        out_specs=row,
        out_shape=jax.ShapeDtypeStruct((t, d), F32),
        compiler_params=_params("parallel"),
        name="ffn",
    )(x, norm, norm, wg, wu, wd)


def _rope(z, cos, sin_signed, first_half):
    rot = jnp.where(first_half, pltpu.roll(z, LANES - A_HEAD_DIM // 2, 1), pltpu.roll(z, A_HEAD_DIM // 2, 1))
    return z * cos + rot * sin_signed


def _proj_kernel(x_ref, g_ref, w_ref, wc_ref, wgu_ref, bg_ref, cos_ref, sin_ref, *rest, k_transposed, layer, n_alias):
    (kf_ref, vf_ref, q1_ref, q2_ref, kb_ref, vt_ref, qg_ref, kg_ref, vg_ref, rg_ref, la_ref) = rest[n_alias:]
    nt = (((1,), (1,)), ((), ()))
    if n_alias == 0:
        for j in range(kf_ref.shape[0]):
            if j != layer:
                kf_ref[j] = jnp.zeros(kf_ref.shape[1:], F32)
                vf_ref[j] = jnp.zeros(vf_ref.shape[1:], F32)
        kf_ref, vf_ref = kf_ref.at[layer], vf_ref.at[layer]
    h = _rms(x_ref[...], g_ref[...], NORM_EPS).astype(BF16)
    z = lax.dot_general(h, w_ref[...], nt, preferred_element_type=F32)
    cos = cos_ref[...]
    sin = sin_ref[...]
    lane = lax.broadcasted_iota(jnp.int32, cos.shape, 1)
    first_half = (lane % A_HEAD_DIM) < (A_HEAD_DIM // 2)
    even_head = lane < A_HEAD_DIM
    scale = A_HEAD_DIM ** -0.5 * LOG2E
    o = 0
    for hh in range(A_HEADS):
        sl = slice(hh * LANES, (hh + 1) * LANES)
        q = _rope(z[:, o + hh * LANES:o + (hh + 1) * LANES], cos, sin, first_half) * scale
        q1_ref[:, sl] = jnp.where(even_head, q, 0.0).astype(BF16)
        q2_ref[:, sl] = jnp.where(even_head, 0.0, q).astype(BF16)
    o += 2 * A_HEADS * A_HEAD_DIM
    for hh in range(A_HEADS):
        sl = slice(hh * LANES, (hh + 1) * LANES)
        k = _rope(z[:, o + hh * LANES:o + (hh + 1) * LANES], cos, sin, first_half)
        if k_transposed:
            kf_ref[sl, :] = k.T
        else:
            kf_ref[:, sl] = k
        kb_ref[:, sl] = k.astype(BF16)
    o += 2 * A_HEADS * A_HEAD_DIM
    va = z[:, o:o + A_WIDTH]
    for hh in range(A_HEADS):
        vf_ref[pl.ds(hh, va.shape[0], stride=A_HEADS), :] = va[:, hh * A_V_DIM:(hh + 1) * A_V_DIM]
        vt_ref[hh * V_ROWS:hh * V_ROWS + A_V_DIM, :] = va[:, hh * A_V_DIM:(hh + 1) * A_V_DIM].T.astype(BF16)
        vt_ref[hh * V_ROWS + A_V_DIM:(hh + 1) * V_ROWS, :] = jnp.ones((V_ROWS - A_V_DIM, va.shape[0]), BF16)
    o += A_WIDTH
    qg_ref[...] = z[:, o:o + B_KEYS] * (B_KEY_DIM ** -0.5)
    o += B_KEYS
    kg_ref[...] = z[:, o:o + B_KEYS]
    o += B_KEYS
    vg_ref[...] = z[:, o:o + B_WIDTH].astype(BF16)
    o += B_WIDTH
    rg_ref[...] = z[:, o:o + B_WIDTH]
    code = lax.dot_general(h, wc_ref[...], nt, preferred_element_type=F32).astype(BF16)
    gate = jnp.dot(code, wgu_ref[...], preferred_element_type=F32) + bg_ref[...]
    la_ref[...] = (jnp.minimum(gate, 0.0) - jnp.log1p(jnp.exp(-jnp.abs(gate)))) * (1.0 / GATE_TAU)


def _proj(x, norm, w_t, wgu, bg, cos, sin, layer, k_transposed, kv_prev):
    t, d = x.shape
    depth = w_t.shape[0]
    tm = min(ROW_TILE, t)
    row = lambda n: pl.BlockSpec((tm, n), lambda i: (i, 0))
    col = lambda n: pl.BlockSpec((n, tm), lambda i: (0, i))
    sds = jax.ShapeDtypeStruct
    n_qk = 2 * A_HEADS * A_HEAD_DIM
    n_main = 2 * n_qk + A_WIDTH + 2 * B_KEYS + 2 * B_WIDTH
    assert n_main % GATE_RANK == 0
    k_shape = (n_qk, t) if k_transposed else (t, n_qk)
    k_block = (n_qk, tm) if k_transposed else (tm, n_qk)
    k_index = (lambda i: (0, i)) if k_transposed else (lambda i: (i, 0))
    v_shape, v_block = (t * A_HEADS, A_V_DIM), (tm * A_HEADS, A_V_DIM)
    if kv_prev is None:
        kv_specs = [pl.BlockSpec((depth,) + k_block, lambda i: (0,) + k_index(i)),
                    pl.BlockSpec((depth,) + v_block, lambda i: (0, i, 0))]
        alias_specs, alias_args, aliases = [], (), {}
    else:
        kv_specs = [pl.BlockSpec((None,) + k_block, lambda i: (layer,) + k_index(i)),
                    pl.BlockSpec((None,) + v_block, lambda i: (layer, i, 0))]
        alias_specs = [pl.BlockSpec(memory_space=pl.ANY)] * 2
        alias_args, aliases = tuple(kv_prev), {8: 0, 9: 1}
    return pl.pallas_call(
        functools.partial(_proj_kernel, k_transposed=k_transposed, layer=layer, n_alias=len(alias_args)),
        grid=(t // tm,),
        in_specs=[row(d), _pick(norm, layer, 2),
                  pl.BlockSpec((None, n_main, d), lambda i: (layer, 0, 0), pipeline_mode=pl.Buffered(1)),
                  pl.BlockSpec((None, GATE_RANK, d), lambda i: (layer, n_main // GATE_RANK, 0),
                               pipeline_mode=pl.Buffered(1)),
                  _pick(wgu, layer), _pick(bg, layer), row(LANES), row(LANES)] + alias_specs,
        out_specs=kv_specs + [row(A_WIDTH), row(A_WIDTH), row(A_WIDTH), col(A_HEADS * V_ROWS),
                              row(B_KEYS), row(B_KEYS), row(B_WIDTH), row(B_WIDTH), row(B_KEYS)],
        out_shape=[sds((depth,) + k_shape, F32), sds((depth,) + v_shape, F32),
                   sds((t, A_WIDTH), BF16), sds((t, A_WIDTH), BF16), sds((t, A_WIDTH), BF16),
                   sds((A_HEADS * V_ROWS, t), BF16),
                   sds((t, B_KEYS), F32), sds((t, B_KEYS), F32), sds((t, B_WIDTH), BF16),
                   sds((t, B_WIDTH), F32), sds((t, B_KEYS), F32)],
        input_output_aliases=aliases,
        compiler_params=_params("parallel"),
        name="mixer_proj",
    )(x, norm, w_t, w_t, wgu, bg, cos, sin, *alias_args)


def _diff_lambda(lp, lam_init):
    a = jnp.sum(lp[0:1] * lp[1:2], axis=-1, keepdims=True)
    b = jnp.sum(lp[2:3] * lp[3:4], axis=-1, keepdims=True)
    return jnp.exp(a) - jnp.exp(b) + lam_init


def _subln(o, g, lam_init):
    return _rms(o, g, SUBLN_EPS) * (1.0 - lam_init)


def _attn_prompt_kernel(qi_ref, kj_ref, fl_ref, q1_ref, q2_ref, k_ref, vt_ref, lp_ref, g_ref,
                        o_ref, m_ref, acc_ref, st_ref, *, lam_init):
    n = pl.program_id(0)
    tk = k_ref.shape[0]
    flags = fl_ref[n]
    n_maps = 2 * A_HEADS
    slots = st_ref.shape[0]
    nt = (((1,), (1,)), ((), ()))

    @pl.when((flags & 1) != 0)
    def _():
        m_ref[...] = jnp.full(m_ref.shape, MASK_VALUE, F32)
        acc_ref[...] = jnp.zeros(acc_ref.shape, F32)

    def step(masked):
        if masked:
            shape = st_ref.shape[1:]
            kpos = kj_ref[n] * tk + lax.broadcasted_iota(jnp.int32, shape, 0)
            qpos = qi_ref[n] * shape[1] + lax.broadcasted_iota(jnp.int32, shape, 1)
            visible = (kpos // CHUNK) <= (qpos // CHUNK)

        def scores(idx):
            hh, c = divmod(idx, 2)
            sl = slice(hh * LANES, (hh + 1) * LANES)
            st = lax.dot_general(k_ref[:, sl], (q1_ref, q2_ref)[c][:, sl], nt, preferred_element_type=F32)
            if masked:
                st = jnp.where(visible, st, MASK_VALUE)
            st_ref[idx % slots] = st

        def accumulate(idx):
            hh = idx // 2
            st = st_ref[idx % slots]
            m_old = m_ref[idx]
            m_new = jnp.maximum(m_old, jnp.max(st, axis=0, keepdims=True))
            m_ref[idx] = m_new
            p = jnp.exp2(st - m_new).astype(BF16)
            pv = jnp.dot(vt_ref[hh * V_ROWS:(hh + 1) * V_ROWS, :], p, preferred_element_type=F32)
            acc_ref[idx] = jnp.exp2(m_old - m_new) * acc_ref[idx] + pv

        ahead = slots - 1
        for idx in range(min(ahead, n_maps)):
            scores(idx)
        for idx in range(n_maps):
            if idx + ahead < n_maps:
                scores(idx + ahead)
            accumulate(idx)

    @pl.when((flags & 2) == 0)
    def _():
        step(False)

    @pl.when((flags & 2) != 0)
    def _():
        step(True)

    @pl.when((flags & 4) != 0)
    def _():
        lam = _diff_lambda(lp_ref[...], lam_init)
        for hh in range(A_HEADS):
            outs = []
            for idx in (2 * hh, 2 * hh + 1):
                outs.append(acc_ref[idx, 0:A_V_DIM, :] * (1.0 / acc_ref[idx, A_V_DIM:A_V_DIM + 1, :]))
            o = (outs[0] - lam * outs[1]).T
            o_ref[:, hh * LANES:(hh + 1) * LANES] = _subln(o, g_ref[...], lam_init).astype(BF16)


def _attn_pairs(t, tq, tk):
    qi, kj, fl = [], [], []
    for i in range(t // tq):
        q_lo, q_hi = (i * tq) // CHUNK, (i * tq + tq - 1) // CHUNK
        js = [j for j in range(t // tk) if (j * tk) // CHUNK <= q_hi]
        for j in js:
            partial = (j * tk + tk - 1) // CHUNK > q_lo
            qi.append(i)
            kj.append(j)
            fl.append((1 if j == js[0] else 0) | (2 if partial else 0) | (4 if j == js[-1] else 0))
    return (np.asarray(qi, np.int32), np.asarray(kj, np.int32), np.asarray(fl, np.int32))


def _attn_prompt(q1, q2, kb, vt, lam_p, subln_g, layer, lam_init):
    t = q1.shape[0]
    tq, tk = min(ATTN_Q_TILE, t), min(ATTN_K_TILE, t)
    qi, kj, fl = _attn_pairs(t, tq, tk)
    grid_spec = pltpu.PrefetchScalarGridSpec(
        num_scalar_prefetch=3,
        grid=(len(qi),),
        in_specs=[
            pl.BlockSpec((tq, A_WIDTH), lambda n, qi, kj, fl: (qi[n], 0)),
            pl.BlockSpec((tq, A_WIDTH), lambda n, qi, kj, fl: (qi[n], 0)),
            pl.BlockSpec((tk, A_WIDTH), lambda n, qi, kj, fl: (kj[n], 0)),
            pl.BlockSpec((A_HEADS * V_ROWS, tk), lambda n, qi, kj, fl: (0, kj[n])),
            _pick(lam_p, layer),
            _pick(subln_g, layer),
        ],
        out_specs=pl.BlockSpec((tq, A_WIDTH), lambda n, qi, kj, fl: (qi[n], 0)),
        scratch_shapes=[pltpu.VMEM((2 * A_HEADS, 1, tq), F32),
                        pltpu.VMEM((2 * A_HEADS, V_ROWS, tq), F32),
                        pltpu.VMEM((ATTN_SCORE_SLOTS, tk, tq), F32)],
    )
    return pl.pallas_call(
        functools.partial(_attn_prompt_kernel, lam_init=lam_init),
        grid_spec=grid_spec,
        out_shape=jax.ShapeDtypeStruct((t, A_WIDTH), BF16),
        compiler_params=_params("arbitrary"),
        name="attn_prompt",
    )(jnp.asarray(qi), jnp.asarray(kj), jnp.asarray(fl), q1, q2, kb, vt, lam_p, subln_g)


def _attn_sample_kernel(q1_ref, q2_ref, kn_ref, vn_ref, ckt_ref, cv_ref, lp_ref, g_ref, o_ref, *, lam_init):
    past = cv_ref.shape[0] // A_HEADS
    tq = q1_ref.shape[0]
    qpos = past + lax.broadcasted_iota(jnp.int32, (tq, past), 0)
    vis_past = (lax.broadcasted_iota(jnp.int32, (tq, past), 1) // CHUNK) <= (qpos // CHUNK)
    qpos_n = lax.broadcasted_iota(jnp.int32, (tq, tq), 0)
    vis_new = (lax.broadcasted_iota(jnp.int32, (tq, tq), 1) + past) // CHUNK <= (qpos_n + past) // CHUNK
    lam = _diff_lambda(lp_ref[...], lam_init)
    nt = (((1,), (1,)), ((), ()))
    for hh in range(A_HEADS):
        sl = slice(hh * LANES, (hh + 1) * LANES)
        ckt = ckt_ref[2 * hh:2 * hh + 2].reshape(2 * A_HEAD_DIM, past).astype(BF16)
        cv = cv_ref[pl.ds(hh, past, stride=A_HEADS), :].astype(BF16)
        kn = kn_ref[:, sl]
        vn = vn_ref[pl.ds(hh, tq, stride=A_HEADS), :].astype(BF16)
        outs = []
        for q_ref in (q1_ref, q2_ref):
            q = q_ref[:, sl]
            s_p = jnp.where(vis_past, jnp.dot(q, ckt, preferred_element_type=F32), MASK_VALUE)
            s_n = jnp.where(vis_new, lax.dot_general(q, kn, nt, preferred_element_type=F32), MASK_VALUE)
            m = jnp.maximum(jnp.max(s_p, axis=-1, keepdims=True), jnp.max(s_n, axis=-1, keepdims=True))
            p_p = jnp.exp2(s_p - m)
            p_n = jnp.exp2(s_n - m)
            l = jnp.sum(p_p, axis=-1, keepdims=True) + jnp.sum(p_n, axis=-1, keepdims=True)
            pv = (jnp.dot(p_p.astype(BF16), cv, preferred_element_type=F32)
                  + jnp.dot(p_n.astype(BF16), vn, preferred_element_type=F32))
            outs.append(pv * (1.0 / l))
        o = outs[0] - lam * outs[1]
        o_ref[:, sl] = _subln(o, g_ref[...], lam_init).astype(BF16)


def _attn_sample(q1, q2, kb, vf, cache_kt, cache_v, layer, lam_p, subln_g, lam_init):
    _, nb, nh2, hd, past = cache_kt.shape
    tq = q1.shape[0] // nb
    new = lambda: pl.BlockSpec((tq, A_WIDTH), lambda b: (b, 0))
    return pl.pallas_call(
        functools.partial(_attn_sample_kernel, lam_init=lam_init),
        grid=(nb,),
        in_specs=[new(), new(), new(), pl.BlockSpec((None, tq * A_HEADS, A_V_DIM), lambda b: (layer, b, 0)),
                  pl.BlockSpec((None, None, nh2, hd, past), lambda b: (layer, b, 0, 0, 0)),
                  pl.BlockSpec((None, None, past * A_HEADS, A_V_DIM), lambda b: (layer, b, 0, 0)),
                  _pick(lam_p, layer), _pick(subln_g, layer)],
        out_specs=new(),
        out_shape=jax.ShapeDtypeStruct((nb * tq, A_WIDTH), BF16),
        compiler_params=_params("parallel"),
        name="attn_sample",
    )(q1, q2, kb, vf, cache_kt, cache_v, lam_p, subln_g)


def _gla_kernel(q_ref, k_ref, v_ref, la_ref, r_ref, s0_ref, g_ref, o_ref, s_ref, st_ref, *, chunk):
    blk = pl.program_id(1)
    tokens = q_ref.shape[0]

    @pl.when(blk == 0)
    def _():
        st_ref[...] = s0_ref[...].T

    row = lax.broadcasted_iota(jnp.int32, (tokens, tokens), 0)
    col = lax.broadcasted_iota(jnp.int32, (tokens, tokens), 1)
    same_chunk = (row // chunk) == (col // chunk)
    causal = jnp.where(same_chunk, col, tokens) <= row
    tri = jnp.where(causal, 1.0, 0.0).astype(BF16)
    ones = jnp.where(same_chunk, 1.0, 0.0).astype(BF16)
    lane = lax.broadcasted_iota(jnp.int32, (tokens, B_KEYS), 1)
    nt = (((1,), (1,)), ((), ()))
    tn = (((0,), (0,)), ((), ()))

    la = la_ref[...]
    la_hi = la.astype(BF16)
    la_lo = (la - la_hi.astype(F32)).astype(BF16)
    b = (jnp.dot(tri, la_hi, preferred_element_type=F32)
         + jnp.dot(tri, la_lo, preferred_element_type=F32))
    total = (jnp.dot(ones, la_hi, preferred_element_type=F32)
             + jnp.dot(ones, la_lo, preferred_element_type=F32))
    qe = q_ref[...] * jnp.exp(b)
    k = k_ref[...]
    ke = (k * jnp.exp(-b)).astype(BF16)
    kd = k * jnp.exp(total - b)
    qe_h, kd_h, o_intra = [], [], []
    for hh in range(B_HEADS):
        in_head = (lane // B_KEY_DIM) == hh
        qe_h.append(jnp.where(in_head, qe, 0.0).astype(BF16))
        kd_h.append(jnp.where(in_head, kd, 0.0).astype(BF16))
        att = lax.dot_general(qe_h[hh], ke, nt, preferred_element_type=F32)
        att = jnp.where(causal, att, 0.0).astype(BF16)
        o_intra.append(jnp.dot(att, v_ref[:, hh * B_VAL_DIM:(hh + 1) * B_VAL_DIM], preferred_element_type=F32))

    st = st_ref[...]
    for c in range(tokens // chunk):
        r = slice(c * chunk, (c + 1) * chunk)
        q_stack = jnp.concatenate([qe_h[hh][r] for hh in range(B_HEADS)], axis=0)
        kd_stack = jnp.concatenate([kd_h[hh][r] for hh in range(B_HEADS)], axis=0)
        v_stack = jnp.concatenate([v_ref[r, hh * B_VAL_DIM:(hh + 1) * B_VAL_DIM] for hh in range(B_HEADS)], axis=0)
        o_inter = lax.dot_general(q_stack, st.astype(BF16), nt, preferred_element_type=F32)
        for hh in range(B_HEADS):
            vs = slice(hh * B_VAL_DIM, (hh + 1) * B_VAL_DIM)
            o = o_intra[hh][r] + o_inter[hh * chunk:(hh + 1) * chunk]
            rg = r_ref[r, vs]
            o_ref[r, vs] = (_rms(o, g_ref[...], SUBLN_EPS) * (rg * jax.nn.sigmoid(rg))).astype(BF16)
        st = (st * jnp.exp(total[c * chunk:c * chunk + 1, :])
              + lax.dot_general(v_stack, kd_stack, tn, preferred_element_type=F32))
    st_ref[...] = st

    @pl.when(blk == pl.num_programs(1) - 1)
    def _():
        s_ref[...] = st_ref[...].T


def _gla(q, k, v, la, rg, s0, s0_layer, g, layer, chunk):
    nb = s0.shape[1]
    t = q.shape[0] // nb
    tb = min(GLA_BLOCK, t)
    nblk = t // tb
    tok = lambda n: pl.BlockSpec((tb, n), lambda b, i: (b * nblk + i, 0))
    state = pl.BlockSpec((None, B_KEYS, B_VAL_DIM), lambda b, i: (b, 0, 0))
    return pl.pallas_call(
        functools.partial(_gla_kernel, chunk=chunk),
        grid=(nb, nblk),
        in_specs=[tok(B_KEYS), tok(B_KEYS), tok(B_WIDTH), tok(B_KEYS), tok(B_WIDTH),
                  pl.BlockSpec((None, None, B_KEYS, B_VAL_DIM), lambda b, i: (s0_layer, b, 0, 0)),
                  _pick(g, layer)],
        out_specs=[tok(B_WIDTH), state],
        out_shape=[jax.ShapeDtypeStruct((nb * t, B_WIDTH), BF16),
                   jax.ShapeDtypeStruct((nb, B_KEYS, B_VAL_DIM), F32)],
        scratch_shapes=[pltpu.VMEM((B_VAL_DIM, B_KEYS), F32)],
        compiler_params=_params("parallel", "arbitrary"),
        name="gla",
    )(q, k, v, la, rg, s0, g)


def _merge_kernel(x_ref, oa_ref, ob_ref, gpre_ref, gpost_ref, wgab_ref, woa_ref, wob_ref, wo_ref, o_ref):
    x = x_ref[...]
    d = x.shape[1]
    h = _rms(x, gpre_ref[...], NORM_EPS).astype(BF16)
    gab = lax.dot_general(h, wgab_ref[0], (((1,), (1,)), ((), ())), preferred_element_type=F32)
    ya = jnp.dot(oa_ref[...], woa_ref[...], preferred_element_type=F32)
    yb = jnp.dot(ob_ref[...], wob_ref[...], preferred_element_type=F32)
    m = jax.nn.sigmoid(gab[:, :d]) * ya + jax.nn.sigmoid(gab[:, d:]) * yb
    y = jnp.dot(m.astype(BF16), wo_ref[...], preferred_element_type=F32)
    o_ref[...] = x + _rms(y, gpost_ref[...], NORM_EPS)


def _merge(x, oa, ob, norm, w_t, woa, wob, wo, layer):
    t, d = x.shape
    tm = min(ROW_TILE, t)
    row = lambda n: pl.BlockSpec((tm, n), lambda i: (i, 0))
    gate_rows = w_t.shape[1] - 2 * d
    w_gates = pl.BlockSpec((pl.Element(1), pl.Element(2 * d), pl.Element(d)), lambda i: (layer, gate_rows, 0),
                           pipeline_mode=pl.Buffered(1))
    return pl.pallas_call(
        _merge_kernel,
        grid=(t // tm,),
        in_specs=[row(d), row(A_WIDTH), row(B_WIDTH), _pick(norm, layer, 2), _pick(norm, layer, 3),
                  w_gates, _pick(woa, layer), _pick(wob, layer), _pick(wo, layer)],
        out_specs=row(d),
        out_shape=jax.ShapeDtypeStruct((t, d), F32),
        compiler_params=_params("parallel"),
        name="mixer_merge",
    )(x, oa, ob, norm, norm, w_t, woa, wob, wo)


def _rope_tables(pos):
    half = A_HEAD_DIM // 2
    inv = jnp.exp(-math.log(ROPE_THETA) * jnp.arange(half, dtype=F32) / half)
    ang = pos.astype(F32)[:, None] * inv[None, :]
    cos, sin = jnp.cos(ang), jnp.sin(ang)
    reps = LANES // A_HEAD_DIM
    return (jnp.tile(jnp.concatenate([cos, cos], axis=-1), (1, reps)),
            jnp.tile(jnp.concatenate([-sin, sin], axis=-1), (1, reps)))


def kernel(x_prompt, x_sample, cache_k, cache_v, state_gla, norm_g, ffn_w_gate, ffn_w_up, ffn_w_down, w_in,
           w_gate_up, b_gate, lambda_p, subln_g, gla_norm_g, w_out_a, w_out_b, w_out):
    depth = w_in.shape[0]
    bp, tp, d = x_prompt.shape
    bs, ts, _ = x_sample.shape
    past = cache_k.shape[2]
    assert bp == 1 and tp % CHUNK == 0

    n_qk = 2 * A_HEADS * A_HEAD_DIM
    assert w_in.shape[2] == 2 * n_qk + A_WIDTH + 2 * B_KEYS + 2 * B_WIDTH + GATE_RANK + 2 * d

    norm = norm_g.reshape(depth, norm_g.shape[1], 1, d)
    wg, wu, wd = ffn_w_gate.astype(BF16), ffn_w_up.astype(BF16), ffn_w_down.astype(BF16)
    w_t = jnp.transpose(w_in, (0, 2, 1)).astype(BF16)
    w_gu = w_gate_up.astype(BF16)
    bg = b_gate[:, None, :]
    woa, wob, wo = w_out_a.astype(BF16), w_out_b.astype(BF16), w_out.astype(BF16)
    sg = subln_g[:, None, :]
    gg = gla_norm_g[:, None, :]

    cos_p, sin_p = _rope_tables(jnp.arange(tp))
    cos_s, sin_s = _rope_tables(jnp.tile(past + jnp.arange(ts), bs))

    cache_kt = jnp.transpose(cache_k, (0, 1, 3, 4, 2))
    cache_vr = cache_v.reshape(depth, bs, past * A_HEADS, A_V_DIM)
    s0_sample = state_gla.reshape(depth, bs, B_KEYS, B_VAL_DIM)
    s0_prompt = jnp.zeros((1, bp, B_KEYS, B_VAL_DIM), F32)

    xp = x_prompt.reshape(tp, d)
    xs = x_sample.reshape(bs * ts, d)
    kv_p = kv_s = None
    states_p, states_s = [], []
    for l in range(depth):
        lam_init = 0.8 - 0.6 * math.exp(-0.3 * l)

        def layer(x, cos, sin, chunk, s0, s0_layer, kv, is_prompt):
            x = _ffn(x, norm, wg, wu, wd, l, 0)
            kf, vf, q1, q2, kb, vt, qg, kg, vg, rg, la = _proj(x, norm, w_t, w_gu, bg, cos, sin, l, is_prompt, kv)
            if is_prompt:
                oa = _attn_prompt(q1, q2, kb, vt, lambda_p, sg, l, lam_init)
            else:
                oa = _attn_sample(q1, q2, kb, vf, cache_kt, cache_vr, l, lambda_p, sg, lam_init)
            ob, s_new = _gla(qg, kg, vg, la, rg, s0, s0_layer, gg, l, chunk)
            x = _merge(x, oa, ob, norm, w_t, woa, wob, wo, l)
            x = _ffn(x, norm, wg, wu, wd, l, 1)
            return x, (kf, vf), s_new

        xp, kv_p, sp = layer(xp, cos_p, sin_p, CHUNK, s0_prompt, 0, kv_p, True)
        xs, kv_s, sn = layer(xs, cos_s, sin_s, ts, s0_sample, l, kv_s, False)
        states_p.append(sp)
        states_s.append(sn)

    k_prompt = jnp.transpose(kv_p[0].reshape(depth, bp, 2 * A_HEADS, A_HEAD_DIM, tp), (0, 1, 4, 2, 3))
    v_prompt = kv_p[1].reshape(depth, bp, tp, A_HEADS, A_V_DIM)
    s_prompt = jnp.stack(states_p).reshape(depth, bp, B_HEADS, B_KEY_DIM, B_VAL_DIM)
    k_sample = kv_s[0].reshape(depth, bs, ts, 2 * A_HEADS, A_HEAD_DIM)
    v_sample = kv_s[1].reshape(depth, bs, ts, A_HEADS, A_V_DIM)
    s_sample = jnp.stack(states_s).reshape(depth, bs, B_HEADS, B_KEY_DIM, B_VAL_DIM)
    return (xp.reshape(bp, tp, d), xs.reshape(bs, ts, d), k_prompt, v_prompt, s_prompt,
            k_sample, v_sample, s_sample)
```

```python
import functools
import math

import jax
import jax.numpy as jnp
import numpy as np
from jax import lax
from jax.experimental import pallas as pl
from jax.experimental.pallas import tpu as pltpu

F32 = jnp.float32
BF16 = jnp.bfloat16

CHUNK = 64
A_HEADS = 4
A_HEAD_DIM = 64
A_V_DIM = 2 * A_HEAD_DIM
A_WIDTH = A_HEADS * A_V_DIM
B_HEADS = 4
B_KEY_DIM = 64
B_VAL_DIM = 128
B_KEYS = B_HEADS * B_KEY_DIM
B_WIDTH = B_HEADS * B_VAL_DIM
GATE_RANK = 16
GATE_TAU = 16.0
ROPE_THETA = 10000.0
NORM_EPS = 1e-6
SUBLN_EPS = 1e-5
LANES = 128
MASK_VALUE = -1e30

ROW_TILE = 512
ATTN_Q_TILE = 1024
ATTN_K_TILE = 1024
ATTN_SCORE_SLOTS = 3
V_ROWS = A_V_DIM + 16
LOG2E = math.log2(math.e)
GLA_BLOCK = 512
VMEM_LIMIT = 56 * 1024 * 1024


def _rms(x, g, eps):
    return x * lax.rsqrt(jnp.mean(x * x, axis=-1, keepdims=True) + eps) * g


def _params(*sem):
    return pltpu.CompilerParams(dimension_semantics=sem, vmem_limit_bytes=VMEM_LIMIT)


def _ffn_kernel(x_ref, gpre_ref, gpost_ref, wg_ref, wu_ref, wd_ref, o_ref):
    x = x_ref[...]
    h = _rms(x, gpre_ref[...], NORM_EPS).astype(BF16)
    g = jnp.dot(h, wg_ref[...], preferred_element_type=F32)
    u = jnp.dot(h, wu_ref[...], preferred_element_type=F32)
    a = (g * jax.nn.sigmoid(g) * u).astype(BF16)
    y = jnp.dot(a, wd_ref[...], preferred_element_type=F32)
    o_ref[...] = x + 0.5 * _rms(y, gpost_ref[...], NORM_EPS)


def _pick(a, *lead):
    rest = a.shape[len(lead):]
    index = tuple(lead) + (0,) * len(rest)
    return pl.BlockSpec((None,) * len(lead) + rest, lambda *_: index, pipeline_mode=pl.Buffered(1))


def _ffn(x, norm, wg, wu, wd, layer, which):
    t, d = x.shape
    tm = min(ROW_TILE, t)
    row = pl.BlockSpec((tm, d), lambda i: (i, 0))
    return pl.pallas_call(
        _ffn_kernel,
        grid=(t // tm,),
        in_specs=[row, _pick(norm, layer, 4 * which), _pick(norm, layer, 4 * which + 1),
                  _pick(wg, layer, which), _pick(wu, layer, which), _pick(wd, layer, which)],
        out_specs=row,
        out_shape=jax.ShapeDtypeStruct((t, d), F32),
        compiler_params=_params("parallel"),
        name="ffn",
    )(x, norm, norm, wg, wu, wd)


def _rope(z, cos, sin_signed, first_half):
    rot = jnp.where(first_half, pltpu.roll(z, LANES - A_HEAD_DIM // 2, 1), pltpu.roll(z, A_HEAD_DIM // 2, 1))
    return z * cos + rot * sin_signed


def _proj_kernel(x_ref, g_ref, w_ref, wc_ref, wgu_ref, bg_ref, cos_ref, sin_ref, *rest, k_transposed, layer, n_alias):
    (kf_ref, vf_ref, q1_ref, q2_ref, kb_ref, vt_ref, qg_ref, kg_ref, vg_ref, rg_ref, la_ref) = rest[n_alias:]
    nt = (((1,), (1,)), ((), ()))
    if n_alias == 0:
        for j in range(kf_ref.shape[0]):
            if j != layer:
                kf_ref[j] = jnp.zeros(kf_ref.shape[1:], F32)
                vf_ref[j] = jnp.zeros(vf_ref.shape[1:], F32)
        kf_ref, vf_ref = kf_ref.at[layer], vf_ref.at[layer]
    h = _rms(x_ref[...], g_ref[...], NORM_EPS).astype(BF16)
    z = lax.dot_general(h, w_ref[...], nt, preferred_element_type=F32)
    cos = cos_ref[...]
    sin = sin_ref[...]
    lane = lax.broadcasted_iota(jnp.int32, cos.shape, 1)
    first_half = (lane % A_HEAD_DIM) < (A_HEAD_DIM // 2)
    even_head = lane < A_HEAD_DIM
    scale = A_HEAD_DIM ** -0.5 * LOG2E
    o = 0
    for hh in range(A_HEADS):
        sl = slice(hh * LANES, (hh + 1) * LANES)
        q = _rope(z[:, o + hh * LANES:o + (hh + 1) * LANES], cos, sin, first_half) * scale
        q1_ref[:, sl] = jnp.where(even_head, q, 0.0).astype(BF16)
        q2_ref[:, sl] = jnp.where(even_head, 0.0, q).astype(BF16)
    o += 2 * A_HEADS * A_HEAD_DIM
    for hh in range(A_HEADS):
        sl = slice(hh * LANES, (hh + 1) * LANES)
        k = _rope(z[:, o + hh * LANES:o + (hh + 1) * LANES], cos, sin, first_half)
        if k_transposed:
            kf_ref[sl, :] = k.T
        else:
            kf_ref[:, sl] = k
        kb_ref[:, sl] = k.astype(BF16)
    o += 2 * A_HEADS * A_HEAD_DIM
    va = z[:, o:o + A_WIDTH]
    for hh in range(A_HEADS):
        vf_ref[pl.ds(hh, va.shape[0], stride=A_HEADS), :] = va[:, hh * A_V_DIM:(hh + 1) * A_V_DIM]
        vt_ref[hh * V_ROWS:hh * V_ROWS + A_V_DIM, :] = va[:, hh * A_V_DIM:(hh + 1) * A_V_DIM].T.astype(BF16)
        vt_ref[hh * V_ROWS + A_V_DIM:(hh + 1) * V_ROWS, :] = jnp.ones((V_ROWS - A_V_DIM, va.shape[0]), BF16)
    o += A_WIDTH
    qg_ref[...] = z[:, o:o + B_KEYS] * (B_KEY_DIM ** -0.5)
    o += B_KEYS
    kg_ref[...] = z[:, o:o + B_KEYS]
    o += B_KEYS
    vg_ref[...] = z[:, o:o + B_WIDTH].astype(BF16)
    o += B_WIDTH
    rg_ref[...] = z[:, o:o + B_WIDTH]
    code = lax.dot_general(h, wc_ref[...], nt, preferred_element_type=F32).astype(BF16)
    gate = jnp.dot(code, wgu_ref[...], preferred_element_type=F32) + bg_ref[...]
    la_ref[...] = (jnp.minimum(gate, 0.0) - jnp.log1p(jnp.exp(-jnp.abs(gate)))) * (1.0 / GATE_TAU)


def _proj(x, norm, w_t, wgu, bg, cos, sin, layer, k_transposed, kv_prev):
    t, d = x.shape
    depth = w_t.shape[0]
    tm = min(ROW_TILE, t)
    row = lambda n: pl.BlockSpec((tm, n), lambda i: (i, 0))
    col = lambda n: pl.BlockSpec((n, tm), lambda i: (0, i))
    sds = jax.ShapeDtypeStruct
    n_qk = 2 * A_HEADS * A_HEAD_DIM
    n_main = 2 * n_qk + A_WIDTH + 2 * B_KEYS + 2 * B_WIDTH
    assert n_main % GATE_RANK == 0
    k_shape = (n_qk, t) if k_transposed else (t, n_qk)
    k_block = (n_qk, tm) if k_transposed else (tm, n_qk)
    k_index = (lambda i: (0, i)) if k_transposed else (lambda i: (i, 0))
    v_shape, v_block = (t * A_HEADS, A_V_DIM), (tm * A_HEADS, A_V_DIM)
    if kv_prev is None:
        kv_specs = [pl.BlockSpec((depth,) + k_block, lambda i: (0,) + k_index(i)),
                    pl.BlockSpec((depth,) + v_block, lambda i: (0, i, 0))]
        alias_specs, alias_args, aliases = [], (), {}
    else:
        kv_specs = [pl.BlockSpec((None,) + k_block, lambda i: (layer,) + k_index(i)),
                    pl.BlockSpec((None,) + v_block, lambda i: (layer, i, 0))]
        alias_specs = [pl.BlockSpec(memory_space=pl.ANY)] * 2
        alias_args, aliases = tuple(kv_prev), {8: 0, 9: 1}
    return pl.pallas_call(
        functools.partial(_proj_kernel, k_transposed=k_transposed, layer=layer, n_alias=len(alias_args)),
        grid=(t // tm,),
        in_specs=[row(d), _pick(norm, layer, 2),
                  pl.BlockSpec((None, n_main, d), lambda i: (layer, 0, 0), pipeline_mode=pl.Buffered(1)),
                  pl.BlockSpec((None, GATE_RANK, d), lambda i: (layer, n_main // GATE_RANK, 0),
                               pipeline_mode=pl.Buffered(1)),
                  _pick(wgu, layer), _pick(bg, layer), row(LANES), row(LANES)] + alias_specs,
        out_specs=kv_specs + [row(A_WIDTH), row(A_WIDTH), row(A_WIDTH), col(A_HEADS * V_ROWS),
                              row(B_KEYS), row(B_KEYS), row(B_WIDTH), row(B_WIDTH), row(B_KEYS)],
        out_shape=[sds((depth,) + k_shape, F32), sds((depth,) + v_shape, F32),
                   sds((t, A_WIDTH), BF16), sds((t, A_WIDTH), BF16), sds((t, A_WIDTH), BF16),
                   sds((A_HEADS * V_ROWS, t), BF16),
                   sds((t, B_KEYS), F32), sds((t, B_KEYS), F32), sds((t, B_WIDTH), BF16),
                   sds((t, B_WIDTH), F32), sds((t, B_KEYS), F32)],
        input_output_aliases=aliases,
        compiler_params=_params("parallel"),
        name="mixer_proj",
    )(x, norm, w_t, w_t, wgu, bg, cos, sin, *alias_args)


def _diff_lambda(lp, lam_init):
    a = jnp.sum(lp[0:1] * lp[1:2], axis=-1, keepdims=True)
    b = jnp.sum(lp[2:3] * lp[3:4], axis=-1, keepdims=True)
    return jnp.exp(a) - jnp.exp(b) + lam_init


def _subln(o, g, lam_init):
    return _rms(o, g, SUBLN_EPS) * (1.0 - lam_init)


def _attn_prompt_kernel(qi_ref, kj_ref, fl_ref, q1_ref, q2_ref, k_ref, vt_ref, lp_ref, g_ref,
                        o_ref, m_ref, acc_ref, st_ref, *, lam_init):
    n = pl.program_id(0)
    tk = k_ref.shape[0]
    flags = fl_ref[n]
    n_maps = 2 * A_HEADS
    slots = st_ref.shape[0]
    nt = (((1,), (1,)), ((), ()))

    @pl.when((flags & 1) != 0)
    def _():
        m_ref[...] = jnp.full(m_ref.shape, MASK_VALUE, F32)
        acc_ref[...] = jnp.zeros(acc_ref.shape, F32)

    def step(masked):
        if masked:
            shape = st_ref.shape[1:]
            kpos = kj_ref[n] * tk + lax.broadcasted_iota(jnp.int32, shape, 0)
            qpos = qi_ref[n] * shape[1] + lax.broadcasted_iota(jnp.int32, shape, 1)
            visible = (kpos // CHUNK) <= (qpos // CHUNK)

        def scores(idx):
            hh, c = divmod(idx, 2)
            sl = slice(hh * LANES, (hh + 1) * LANES)
            st = lax.dot_general(k_ref[:, sl], (q1_ref, q2_ref)[c][:, sl], nt, preferred_element_type=F32)
            if masked:
                st = jnp.where(visible, st, MASK_VALUE)
            st_ref[idx % slots] = st

        def accumulate(idx):
            hh = idx // 2
            st = st_ref[idx % slots]
            m_old = m_ref[idx]
            m_new = jnp.maximum(m_old, jnp.max(st, axis=0, keepdims=True))
            m_ref[idx] = m_new
            p = jnp.exp2(st - m_new).astype(BF16)
            pv = jnp.dot(vt_ref[hh * V_ROWS:(hh + 1) * V_ROWS, :], p, preferred_element_type=F32)
            acc_ref[idx] = jnp.exp2(m_old - m_new) * acc_ref[idx] + pv

        ahead = slots - 1
        for idx in range(min(ahead, n_maps)):
            scores(idx)
        for idx in range(n_maps):
            if idx + ahead < n_maps:
                scores(idx + ahead)
            accumulate(idx)

    @pl.when((flags & 2) == 0)
    def _():
        step(False)

    @pl.when((flags & 2) != 0)
    def _():
        step(True)

    @pl.when((flags & 4) != 0)
    def _():
        lam = _diff_lambda(lp_ref[...], lam_init)
        for hh in range(A_HEADS):
            outs = []
            for idx in (2 * hh, 2 * hh + 1):
                outs.append(acc_ref[idx, 0:A_V_DIM, :] * (1.0 / acc_ref[idx, A_V_DIM:A_V_DIM + 1, :]))
            o = (outs[0] - lam * outs[1]).T
            o_ref[:, hh * LANES:(hh + 1) * LANES] = _subln(o, g_ref[...], lam_init).astype(BF16)


def _attn_pairs(t, tq, tk):
    qi, kj, fl = [], [], []
    for i in range(t // tq):
        q_lo, q_hi = (i * tq) // CHUNK, (i * tq + tq - 1) // CHUNK
        js = [j for j in range(t // tk) if (j * tk) // CHUNK <= q_hi]
        for j in js:
            partial = (j * tk + tk - 1) // CHUNK > q_lo
            qi.append(i)
            kj.append(j)
            fl.append((1 if j == js[0] else 0) | (2 if partial else 0) | (4 if j == js[-1] else 0))
    return (np.asarray(qi, np.int32), np.asarray(kj, np.int32), np.asarray(fl, np.int32))


def _attn_prompt(q1, q2, kb, vt, lam_p, subln_g, layer, lam_init):
    t = q1.shape[0]
    tq, tk = min(ATTN_Q_TILE, t), min(ATTN_K_TILE, t)
    qi, kj, fl = _attn_pairs(t, tq, tk)
    grid_spec = pltpu.PrefetchScalarGridSpec(
        num_scalar_prefetch=3,
        grid=(len(qi),),
        in_specs=[
            pl.BlockSpec((tq, A_WIDTH), lambda n, qi, kj, fl: (qi[n], 0)),
            pl.BlockSpec((tq, A_WIDTH), lambda n, qi, kj, fl: (qi[n], 0)),
            pl.BlockSpec((tk, A_WIDTH), lambda n, qi, kj, fl: (kj[n], 0)),
            pl.BlockSpec((A_HEADS * V_ROWS, tk), lambda n, qi, kj, fl: (0, kj[n])),
            _pick(lam_p, layer),
            _pick(subln_g, layer),
        ],
        out_specs=pl.BlockSpec((tq, A_WIDTH), lambda n, qi, kj, fl: (qi[n], 0)),
        scratch_shapes=[pltpu.VMEM((2 * A_HEADS, 1, tq), F32),
                        pltpu.VMEM((2 * A_HEADS, V_ROWS, tq), F32),
                        pltpu.VMEM((ATTN_SCORE_SLOTS, tk, tq), F32)],
    )
    return pl.pallas_call(
        functools.partial(_attn_prompt_kernel, lam_init=lam_init),
        grid_spec=grid_spec,
        out_shape=jax.ShapeDtypeStruct((t, A_WIDTH), BF16),
        compiler_params=_params("arbitrary"),
        name="attn_prompt",
    )(jnp.asarray(qi), jnp.asarray(kj), jnp.asarray(fl), q1, q2, kb, vt, lam_p, subln_g)


def _attn_sample_kernel(q1_ref, q2_ref, kn_ref, vn_ref, ckt_ref, cv_ref, lp_ref, g_ref, o_ref, *, lam_init):
    past = cv_ref.shape[0] // A_HEADS
    tq = q1_ref.shape[0]
    qpos = past + lax.broadcasted_iota(jnp.int32, (tq, past), 0)
    vis_past = (lax.broadcasted_iota(jnp.int32, (tq, past), 1) // CHUNK) <= (qpos // CHUNK)
    qpos_n = lax.broadcasted_iota(jnp.int32, (tq, tq), 0)
    vis_new = (lax.broadcasted_iota(jnp.int32, (tq, tq), 1) + past) // CHUNK <= (qpos_n + past) // CHUNK
    lam = _diff_lambda(lp_ref[...], lam_init)
    nt = (((1,), (1,)), ((), ()))
    scores = []
    for hh in range(A_HEADS):
        sl = slice(hh * LANES, (hh + 1) * LANES)
        ckt = ckt_ref[2 * hh:2 * hh + 2].reshape(2 * A_HEAD_DIM, past).astype(BF16)
        kn = kn_ref[:, sl]
        for q_ref in (q1_ref, q2_ref):
            q = q_ref[:, sl]
            scores.append((jnp.where(vis_past, jnp.dot(q, ckt, preferred_element_type=F32), MASK_VALUE),
                           jnp.where(vis_new, lax.dot_general(q, kn, nt, preferred_element_type=F32), MASK_VALUE)))
    probs = []
    for s_p, s_n in scores:
        m = jnp.maximum(jnp.max(s_p, axis=-1, keepdims=True), jnp.max(s_n, axis=-1, keepdims=True))
        p_p = jnp.exp2(s_p - m)
        p_n = jnp.exp2(s_n - m)
        l = jnp.sum(p_p, axis=-1, keepdims=True) + jnp.sum(p_n, axis=-1, keepdims=True)
        probs.append((p_p.astype(BF16), p_n.astype(BF16), 1.0 / l))
    for hh in range(A_HEADS):
        cv = cv_ref[pl.ds(hh, past, stride=A_HEADS), :].astype(BF16)
        vn = vn_ref[pl.ds(hh, tq, stride=A_HEADS), :].astype(BF16)
        outs = []
        for p_p, p_n, inv_l in probs[2 * hh:2 * hh + 2]:
            pv = jnp.dot(p_p, cv, preferred_element_type=F32) + jnp.dot(p_n, vn, preferred_element_type=F32)
            outs.append(pv * inv_l)
        o = outs[0] - lam * outs[1]
        o_ref[:, hh * LANES:(hh + 1) * LANES] = _subln(o, g_ref[...], lam_init).astype(BF16)


def _attn_sample(q1, q2, kb, vf, cache_kt, cache_v, layer, lam_p, subln_g, lam_init):
    _, nb, nh2, hd, past = cache_kt.shape
    tq = q1.shape[0] // nb
    new = lambda: pl.BlockSpec((tq, A_WIDTH), lambda b: (b, 0))
    return pl.pallas_call(
        functools.partial(_attn_sample_kernel, lam_init=lam_init),
        grid=(nb,),
        in_specs=[new(), new(), new(), pl.BlockSpec((None, tq * A_HEADS, A_V_DIM), lambda b: (layer, b, 0)),
                  pl.BlockSpec((None, None, nh2, hd, past), lambda b: (layer, b, 0, 0, 0)),
                  pl.BlockSpec((None, None, past * A_HEADS, A_V_DIM), lambda b: (layer, b, 0, 0)),
                  _pick(lam_p, layer), _pick(subln_g, layer)],
        out_specs=new(),
        out_shape=jax.ShapeDtypeStruct((nb * tq, A_WIDTH), BF16),
        compiler_params=_params("parallel"),
        name="attn_sample",
    )(q1, q2, kb, vf, cache_kt, cache_v, lam_p, subln_g)


def _gla_kernel(q_ref, k_ref, v_ref, la_ref, r_ref, s0_ref, g_ref, o_ref, s_ref, st_ref, *, chunk, carry):
    blk = pl.program_id(1)
    tokens = q_ref.shape[0]

    if carry:
        @pl.when(blk == 0)
        def _():
            st_ref[...] = s0_ref[...].T

    row = lax.broadcasted_iota(jnp.int32, (tokens, tokens), 0)
    col = lax.broadcasted_iota(jnp.int32, (tokens, tokens), 1)
    same_chunk = (row // chunk) == (col // chunk)
    causal = jnp.where(same_chunk, col, tokens) <= row
    tri = jnp.where(causal, 1.0, 0.0).astype(BF16)
    ones = jnp.where(same_chunk, 1.0, 0.0).astype(BF16)
    lane = lax.broadcasted_iota(jnp.int32, (tokens, B_KEYS), 1)
    nt = (((1,), (1,)), ((), ()))
    tn = (((0,), (0,)), ((), ()))

    la = la_ref[...]
    la_hi = la.astype(BF16)
    la_lo = (la - la_hi.astype(F32)).astype(BF16)
    b = (jnp.dot(tri, la_hi, preferred_element_type=F32)
         + jnp.dot(tri, la_lo, preferred_element_type=F32))
    total = (jnp.dot(ones, la_hi, preferred_element_type=F32)
             + jnp.dot(ones, la_lo, preferred_element_type=F32))
    qe = q_ref[...] * jnp.exp(b)
    k = k_ref[...]
    ke = (k * jnp.exp(-b)).astype(BF16)
    kd = k * jnp.exp(total - b)
    qe_h, kd_h, o_intra = [], [], []
    for hh in range(B_HEADS):
        in_head = (lane // B_KEY_DIM) == hh
        qe_h.append(jnp.where(in_head, qe, 0.0).astype(BF16))
        kd_h.append(jnp.where(in_head, kd, 0.0).astype(BF16))
        att = lax.dot_general(qe_h[hh], ke, nt, preferred_element_type=F32)
        att = jnp.where(causal, att, 0.0).astype(BF16)
        o_intra.append(jnp.dot(att, v_ref[:, hh * B_VAL_DIM:(hh + 1) * B_VAL_DIM], preferred_element_type=F32))

    if carry:
        st = st_ref[...]
    for c in range(tokens // chunk):
        r = slice(c * chunk, (c + 1) * chunk)
        if not carry:
            st = s0_ref[c].T
        q_stack = jnp.concatenate([qe_h[hh][r] for hh in range(B_HEADS)], axis=0)
        kd_stack = jnp.concatenate([kd_h[hh][r] for hh in range(B_HEADS)], axis=0)
        v_stack = jnp.concatenate([v_ref[r, hh * B_VAL_DIM:(hh + 1) * B_VAL_DIM] for hh in range(B_HEADS)], axis=0)
        o_inter = lax.dot_general(q_stack, st.astype(BF16), nt, preferred_element_type=F32)
        for hh in range(B_HEADS):
            vs = slice(hh * B_VAL_DIM, (hh + 1) * B_VAL_DIM)
            o = o_intra[hh][r] + o_inter[hh * chunk:(hh + 1) * chunk]
            rg = r_ref[r, vs]
            o_ref[r, vs] = (_rms(o, g_ref[...], SUBLN_EPS) * (rg * jax.nn.sigmoid(rg))).astype(BF16)
        st = (st * jnp.exp(total[c * chunk:c * chunk + 1, :])
              + lax.dot_general(v_stack, kd_stack, tn, preferred_element_type=F32))
        if not carry:
            s_ref[c] = st.T
    if carry:
        st_ref[...] = st

        @pl.when(blk == pl.num_programs(1) - 1)
        def _():
            s_ref[...] = st_ref[...].T


def _gla(q, k, v, la, rg, s0, s0_layer, g, layer, chunk):
    nb = s0.shape[1]
    t = q.shape[0] // nb
    carry = t > chunk
    if carry:
        tb = min(GLA_BLOCK, t)
        grid = (nb, t // tb)
        state = pl.BlockSpec((None, B_KEYS, B_VAL_DIM), lambda b, i: (b, 0, 0))
        state_in = pl.BlockSpec((None, None, B_KEYS, B_VAL_DIM), lambda b, i: (s0_layer, b, 0, 0))
    else:
        assert t == chunk
        streams = math.gcd(nb, max(1, GLA_BLOCK // t))
        tb = streams * t
        grid = (nb // streams, 1)
        state = pl.BlockSpec((streams, B_KEYS, B_VAL_DIM), lambda b, i: (b, 0, 0))
        state_in = pl.BlockSpec((None, streams, B_KEYS, B_VAL_DIM), lambda b, i: (s0_layer, b, 0, 0))
    nblk = grid[1]
    tok = lambda n: pl.BlockSpec((tb, n), lambda b, i: (b * nblk + i, 0))
    return pl.pallas_call(
        functools.partial(_gla_kernel, chunk=chunk, carry=carry),
        grid=grid,
        in_specs=[tok(B_KEYS), tok(B_KEYS), tok(B_WIDTH), tok(B_KEYS), tok(B_WIDTH), state_in, _pick(g, layer)],
        out_specs=[tok(B_WIDTH), state],
        out_shape=[jax.ShapeDtypeStruct((nb * t, B_WIDTH), BF16),
                   jax.ShapeDtypeStruct((nb, B_KEYS, B_VAL_DIM), F32)],
        scratch_shapes=[pltpu.VMEM((B_VAL_DIM, B_KEYS), F32)],
        compiler_params=_params("parallel", "arbitrary"),
        name="gla",
    )(q, k, v, la, rg, s0, g)


def _merge_kernel(x_ref, oa_ref, ob_ref, gpre_ref, gpost_ref, wgab_ref, woa_ref, wob_ref, wo_ref, o_ref):
    x = x_ref[...]
    d = x.shape[1]
    h = _rms(x, gpre_ref[...], NORM_EPS).astype(BF16)
    gab = lax.dot_general(h, wgab_ref[0], (((1,), (1,)), ((), ())), preferred_element_type=F32)
    ya = jnp.dot(oa_ref[...], woa_ref[...], preferred_element_type=F32)
    yb = jnp.dot(ob_ref[...], wob_ref[...], preferred_element_type=F32)
    m = jax.nn.sigmoid(gab[:, :d]) * ya + jax.nn.sigmoid(gab[:, d:]) * yb
    y = jnp.dot(m.astype(BF16), wo_ref[...], preferred_element_type=F32)
    o_ref[...] = x + _rms(y, gpost_ref[...], NORM_EPS)


def _merge(x, oa, ob, norm, w_t, woa, wob, wo, layer):
    t, d = x.shape
    tm = min(ROW_TILE, t)
    row = lambda n: pl.BlockSpec((tm, n), lambda i: (i, 0))
    gate_rows = w_t.shape[1] - 2 * d
    w_gates = pl.BlockSpec((pl.Element(1), pl.Element(2 * d), pl.Element(d)), lambda i: (layer, gate_rows, 0),
                           pipeline_mode=pl.Buffered(1))
    return pl.pallas_call(
        _merge_kernel,
        grid=(t // tm,),
        in_specs=[row(d), row(A_WIDTH), row(B_WIDTH), _pick(norm, layer, 2), _pick(norm, layer, 3),
                  w_gates, _pick(woa, layer), _pick(wob, layer), _pick(wo, layer)],
        out_specs=row(d),
        out_shape=jax.ShapeDtypeStruct((t, d), F32),
        compiler_params=_params("parallel"),
        name="mixer_merge",
    )(x, oa, ob, norm, norm, w_t, woa, wob, wo)


def _rope_tables(pos):
    half = A_HEAD_DIM // 2
    inv = jnp.exp(-math.log(ROPE_THETA) * jnp.arange(half, dtype=F32) / half)
    ang = pos.astype(F32)[:, None] * inv[None, :]
    cos, sin = jnp.cos(ang), jnp.sin(ang)
    reps = LANES // A_HEAD_DIM
    return (jnp.tile(jnp.concatenate([cos, cos], axis=-1), (1, reps)),
            jnp.tile(jnp.concatenate([-sin, sin], axis=-1), (1, reps)))


def kernel(x_prompt, x_sample, cache_k, cache_v, state_gla, norm_g, ffn_w_gate, ffn_w_up, ffn_w_down, w_in,
           w_gate_up, b_gate, lambda_p, subln_g, gla_norm_g, w_out_a, w_out_b, w_out):
    depth = w_in.shape[0]
    bp, tp, d = x_prompt.shape
    bs, ts, _ = x_sample.shape
    past = cache_k.shape[2]
    assert bp == 1 and tp % CHUNK == 0

    n_qk = 2 * A_HEADS * A_HEAD_DIM
    assert w_in.shape[2] == 2 * n_qk + A_WIDTH + 2 * B_KEYS + 2 * B_WIDTH + GATE_RANK + 2 * d

    norm = norm_g.reshape(depth, norm_g.shape[1], 1, d)
    wg, wu, wd = ffn_w_gate.astype(BF16), ffn_w_up.astype(BF16), ffn_w_down.astype(BF16)
    w_t = jnp.transpose(w_in, (0, 2, 1)).astype(BF16)
    w_gu = w_gate_up.astype(BF16)
    bg = b_gate[:, None, :]
    woa, wob, wo = w_out_a.astype(BF16), w_out_b.astype(BF16), w_out.astype(BF16)
    sg = subln_g[:, None, :]
    gg = gla_norm_g[:, None, :]

    cos_p, sin_p = _rope_tables(jnp.arange(tp))
    cos_s, sin_s = _rope_tables(jnp.tile(past + jnp.arange(ts), bs))

    cache_kt = jnp.transpose(cache_k, (0, 1, 3, 4, 2))
    cache_vr = cache_v.reshape(depth, bs, past * A_HEADS, A_V_DIM)
    s0_sample = state_gla.reshape(depth, bs, B_KEYS, B_VAL_DIM)
    s0_prompt = jnp.zeros((1, bp, B_KEYS, B_VAL_DIM), F32)

    xp = x_prompt.reshape(tp, d)
    xs = x_sample.reshape(bs * ts, d)
    kv_p = kv_s = None
    states_p, states_s = [], []
    for l in range(depth):
        lam_init = 0.8 - 0.6 * math.exp(-0.3 * l)

        def layer(x, cos, sin, chunk, s0, s0_layer, kv, is_prompt):
            x = _ffn(x, norm, wg, wu, wd, l, 0)
            kf, vf, q1, q2, kb, vt, qg, kg, vg, rg, la = _proj(x, norm, w_t, w_gu, bg, cos, sin, l, is_prompt, kv)
            if is_prompt:
                oa = _attn_prompt(q1, q2, kb, vt, lambda_p, sg, l, lam_init)
            else:
                oa = _attn_sample(q1, q2, kb, vf, cache_kt, cache_vr, l, lambda_p, sg, lam_init)
            ob, s_new = _gla(qg, kg, vg, la, rg, s0, s0_layer, gg, l, chunk)
            x = _merge(x, oa, ob, norm, w_t, woa, wob, wo, l)
            x = _ffn(x, norm, wg, wu, wd, l, 1)
            return x, (kf, vf), s_new

        xp, kv_p, sp = layer(xp, cos_p, sin_p, CHUNK, s0_prompt, 0, kv_p, True)
        xs, kv_s, sn = layer(xs, cos_s, sin_s, ts, s0_sample, l, kv_s, False)
        states_p.append(sp)
        states_s.append(sn)

    k_prompt = jnp.transpose(kv_p[0].reshape(depth, bp, 2 * A_HEADS, A_HEAD_DIM, tp), (0, 1, 4, 2, 3))
    v_prompt = kv_p[1].reshape(depth, bp, tp, A_HEADS, A_V_DIM)
    s_prompt = jnp.stack(states_p).reshape(depth, bp, B_HEADS, B_KEY_DIM, B_VAL_DIM)
    k_sample = kv_s[0].reshape(depth, bs, ts, 2 * A_HEADS, A_HEAD_DIM)
    v_sample = kv_s[1].reshape(depth, bs, ts, A_HEADS, A_V_DIM)
    s_sample = jnp.stack(states_s).reshape(depth, bs, B_HEADS, B_KEY_DIM, B_VAL_DIM)
    return (xp.reshape(bp, tp, d), xs.reshape(bs, ts, d), k_prompt, v_prompt, s_prompt,
            k_sample, v_sample, s_sample)
```

```python
import functools
import math

import jax
import jax.numpy as jnp
import numpy as np
from jax import lax
from jax.experimental import pallas as pl
from jax.experimental.pallas import tpu as pltpu

F32 = jnp.float32
BF16 = jnp.bfloat16

CHUNK = 64
A_HEADS = 4
A_HEAD_DIM = 64
A_V_DIM = 2 * A_HEAD_DIM
A_WIDTH = A_HEADS * A_V_DIM
B_HEADS = 4
B_KEY_DIM = 64
B_VAL_DIM = 128
B_KEYS = B_HEADS * B_KEY_DIM
B_WIDTH = B_HEADS * B_VAL_DIM
GATE_RANK = 16
GATE_TAU = 16.0
ROPE_THETA = 10000.0
NORM_EPS = 1e-6
SUBLN_EPS = 1e-5
LANES = 128
MASK_VALUE = -1e30

ROW_TILE = 512
FFN_ROW_TILE = 1024
ROW_GROUP = 256
MERGE_ROW_TILE = 1024
MERGE_SPLIT = 2
ATTN_Q_TILE = 512
ATTN_K_TILE = 1024
ATTN_SCORE_SLOTS = 3
V_ROWS = A_V_DIM + 16
LOG2E = math.log2(math.e)
GLA_BLOCK = 1024
GLA_SPAN = 256
VMEM_LIMIT = 56 * 1024 * 1024


def _rms(x, g, eps):
    return x * lax.rsqrt(jnp.mean(x * x, axis=-1, keepdims=True) + eps) * g


def _params(*sem):
    return pltpu.CompilerParams(dimension_semantics=sem, vmem_limit_bytes=VMEM_LIMIT)


def _row_groups(rows, group):
    return [slice(i, i + group) for i in range(0, rows, group)]


def _ffn_kernel(x_ref, gpre_ref, gpost_ref, wg_ref, wu_ref, wd_ref, o_ref):
    groups = _row_groups(x_ref.shape[0], min(x_ref.shape[0], ROW_GROUP))
    gate_up = []
    for r in groups:
        h = _rms(x_ref[r, :], gpre_ref[...], NORM_EPS).astype(BF16)
        gate_up.append((jnp.dot(h, wg_ref[...], preferred_element_type=F32),
                        jnp.dot(h, wu_ref[...], preferred_element_type=F32)))
    down = []
    for g, u in gate_up:
        a = (g * jax.nn.sigmoid(g) * u).astype(BF16)
        down.append(jnp.dot(a, wd_ref[...], preferred_element_type=F32))
    for r, y in zip(groups, down):
        o_ref[r, :] = x_ref[r, :] + 0.5 * _rms(y, gpost_ref[...], NORM_EPS)


def _pick(a, *lead):
    rest = a.shape[len(lead):]
    index = tuple(lead) + (0,) * len(rest)
    return pl.BlockSpec((None,) * len(lead) + rest, lambda *_: index, pipeline_mode=pl.Buffered(1))


def _ffn(x, norm, wg, wu, wd, layer, which):
    t, d = x.shape
    tm = min(FFN_ROW_TILE, t)
    row = pl.BlockSpec((tm, d), lambda i: (i, 0))
    return pl.pallas_call(
        _ffn_kernel,
        grid=(t // tm,),
        in_specs=[row, _pick(norm, layer, 4 * which), _pick(norm, layer, 4 * which + 1),
                  _pick(wg, layer, which), _pick(wu, layer, which), _pick(wd, layer, which)],
        out_specs=row,
        out_shape=jax.ShapeDtypeStruct((t, d), F32),
        compiler_params=_params("parallel"),
        name="ffn",
    )(x, norm, norm, wg, wu, wd)


def _rope(z, cos, sin_signed, first_half):
    rot = jnp.where(first_half, pltpu.roll(z, LANES - A_HEAD_DIM // 2, 1), pltpu.roll(z, A_HEAD_DIM // 2, 1))
    return z * cos + rot * sin_signed


def _proj_kernel(x_ref, g_ref, w_ref, wc_ref, wgu_ref, bg_ref, cos_ref, sin_ref, *rest, k_transposed, layer, n_alias):
    (kf_ref, vf_ref, q1_ref, q2_ref, kb_ref, vt_ref, qg_ref, kg_ref, vg_ref, rg_ref, la_ref) = rest[n_alias:]
    nt = (((1,), (1,)), ((), ()))
    if n_alias == 0:
        for j in range(kf_ref.shape[0]):
            if j != layer:
                kf_ref[j] = jnp.zeros(kf_ref.shape[1:], F32)
                vf_ref[j] = jnp.zeros(vf_ref.shape[1:], F32)
        kf_ref, vf_ref = kf_ref.at[layer], vf_ref.at[layer]
    h = _rms(x_ref[...], g_ref[...], NORM_EPS).astype(BF16)
    z = lax.dot_general(h, w_ref[...], nt, preferred_element_type=F32)
    cos = cos_ref[...]
    sin = sin_ref[...]
    lane = lax.broadcasted_iota(jnp.int32, cos.shape, 1)
    first_half = (lane % A_HEAD_DIM) < (A_HEAD_DIM // 2)
    even_head = lane < A_HEAD_DIM
    scale = A_HEAD_DIM ** -0.5 * LOG2E
    o = 0
    for hh in range(A_HEADS):
        sl = slice(hh * LANES, (hh + 1) * LANES)
        q = _rope(z[:, o + hh * LANES:o + (hh + 1) * LANES], cos, sin, first_half) * scale
        q1_ref[:, sl] = jnp.where(even_head, q, 0.0).astype(BF16)
        q2_ref[:, sl] = jnp.where(even_head, 0.0, q).astype(BF16)
    o += 2 * A_HEADS * A_HEAD_DIM
    for hh in range(A_HEADS):
        sl = slice(hh * LANES, (hh + 1) * LANES)
        k = _rope(z[:, o + hh * LANES:o + (hh + 1) * LANES], cos, sin, first_half)
        if k_transposed:
            kf_ref[sl, :] = k.T
        else:
            kf_ref[:, sl] = k
        kb_ref[:, sl] = k.astype(BF16)
    o += 2 * A_HEADS * A_HEAD_DIM
    va = z[:, o:o + A_WIDTH]
    for hh in range(A_HEADS):
        vf_ref[pl.ds(hh, va.shape[0], stride=A_HEADS), :] = va[:, hh * A_V_DIM:(hh + 1) * A_V_DIM]
        vt_ref[hh * V_ROWS:hh * V_ROWS + A_V_DIM, :] = va[:, hh * A_V_DIM:(hh + 1) * A_V_DIM].T.astype(BF16)
        vt_ref[hh * V_ROWS + A_V_DIM:(hh + 1) * V_ROWS, :] = jnp.ones((V_ROWS - A_V_DIM, va.shape[0]), BF16)
    o += A_WIDTH
    qg_ref[...] = z[:, o:o + B_KEYS] * (B_KEY_DIM ** -0.5)
    o += B_KEYS
    kg_ref[...] = z[:, o:o + B_KEYS]
    o += B_KEYS
    vg_ref[...] = z[:, o:o + B_WIDTH].astype(BF16)
    o += B_WIDTH
    rg_ref[...] = z[:, o:o + B_WIDTH]
    code = lax.dot_general(h, wc_ref[...], nt, preferred_element_type=F32).astype(BF16)
    gate = jnp.dot(code, wgu_ref[...], preferred_element_type=F32) + bg_ref[...]
    la_ref[...] = (jnp.minimum(gate, 0.0) - jnp.log1p(jnp.exp(-jnp.abs(gate)))) * (1.0 / GATE_TAU)


def _proj(x, norm, w_t, wgu, bg, cos, sin, layer, k_transposed, kv_prev):
    t, d = x.shape
    depth = w_t.shape[0]
    tm = min(ROW_TILE, t)
    row = lambda n: pl.BlockSpec((tm, n), lambda i: (i, 0))
    col = lambda n: pl.BlockSpec((n, tm), lambda i: (0, i))
    sds = jax.ShapeDtypeStruct
    n_qk = 2 * A_HEADS * A_HEAD_DIM
    n_main = 2 * n_qk + A_WIDTH + 2 * B_KEYS + 2 * B_WIDTH
    assert n_main % GATE_RANK == 0
    k_shape = (n_qk, t) if k_transposed else (t, n_qk)
    k_block = (n_qk, tm) if k_transposed else (tm, n_qk)
    k_index = (lambda i: (0, i)) if k_transposed else (lambda i: (i, 0))
    v_shape, v_block = (t * A_HEADS, A_V_DIM), (tm * A_HEADS, A_V_DIM)
    if kv_prev is None:
        kv_specs = [pl.BlockSpec((depth,) + k_block, lambda i: (0,) + k_index(i)),
                    pl.BlockSpec((depth,) + v_block, lambda i: (0, i, 0))]
        alias_specs, alias_args, aliases = [], (), {}
    else:
        kv_specs = [pl.BlockSpec((None,) + k_block, lambda i: (layer,) + k_index(i)),
                    pl.BlockSpec((None,) + v_block, lambda i: (layer, i, 0))]
        alias_specs = [pl.BlockSpec(memory_space=pl.ANY)] * 2
        alias_args, aliases = tuple(kv_prev), {8: 0, 9: 1}
    return pl.pallas_call(
        functools.partial(_proj_kernel, k_transposed=k_transposed, layer=layer, n_alias=len(alias_args)),
        grid=(t // tm,),
        in_specs=[row(d), _pick(norm, layer, 2),
                  pl.BlockSpec((None, n_main, d), lambda i: (layer, 0, 0), pipeline_mode=pl.Buffered(1)),
                  pl.BlockSpec((None, GATE_RANK, d), lambda i: (layer, n_main // GATE_RANK, 0),
                               pipeline_mode=pl.Buffered(1)),
                  _pick(wgu, layer), _pick(bg, layer), row(LANES), row(LANES)] + alias_specs,
        out_specs=kv_specs + [row(A_WIDTH), row(A_WIDTH), row(A_WIDTH), col(A_HEADS * V_ROWS),
                              row(B_KEYS), row(B_KEYS), row(B_WIDTH), row(B_WIDTH), row(B_KEYS)],
        out_shape=[sds((depth,) + k_shape, F32), sds((depth,) + v_shape, F32),
                   sds((t, A_WIDTH), BF16), sds((t, A_WIDTH), BF16), sds((t, A_WIDTH), BF16),
                   sds((A_HEADS * V_ROWS, t), BF16),
                   sds((t, B_KEYS), F32), sds((t, B_KEYS), F32), sds((t, B_WIDTH), BF16),
                   sds((t, B_WIDTH), F32), sds((t, B_KEYS), F32)],
        input_output_aliases=aliases,
        compiler_params=_params("parallel"),
        name="mixer_proj",
    )(x, norm, w_t, w_t, wgu, bg, cos, sin, *alias_args)


def _diff_lambda(lp, lam_init):
    a = jnp.sum(lp[0:1] * lp[1:2], axis=-1, keepdims=True)
    b = jnp.sum(lp[2:3] * lp[3:4], axis=-1, keepdims=True)
    return jnp.exp(a) - jnp.exp(b) + lam_init


def _subln(o, g, lam_init):
    return _rms(o, g, SUBLN_EPS) * (1.0 - lam_init)


def _attn_prompt_kernel(qi_ref, kj_ref, fl_ref, q1_ref, q2_ref, k_ref, vt_ref, lp_ref, g_ref,
                        o_ref, m_ref, acc_ref, st_ref, *, lam_init):
    n = pl.program_id(0)
    tk = k_ref.shape[0]
    flags = fl_ref[n]
    n_maps = 2 * A_HEADS
    slots = st_ref.shape[0]
    nt = (((1,), (1,)), ((), ()))

    @pl.when((flags & 1) != 0)
    def _():
        m_ref[...] = jnp.full(m_ref.shape, MASK_VALUE, F32)
        acc_ref[...] = jnp.zeros(acc_ref.shape, F32)

    def step(masked):
        if masked:
            tq = q1_ref.shape[0]
            n_chunks = tk // CHUNK
            assert n_chunks <= A_HEAD_DIM
            lane_k = lax.broadcasted_iota(jnp.int32, (tk, LANES), 1)
            chunk_k = lax.broadcasted_iota(jnp.int32, (tk, LANES), 0) // CHUNK
            lane_q = lax.broadcasted_iota(jnp.int32, (tq, LANES), 1)
            chunk_q = (qi_ref[n] * tq) // CHUNK + lax.broadcasted_iota(jnp.int32, (tq, LANES), 0) // CHUNK
            first_chunk = (kj_ref[n] * tk) // CHUNK
            keep, onehot, penalty = [], [], []
            for c in range(2):
                off = A_HEAD_DIM * (1 - c)
                own = (lane_k < A_HEAD_DIM) if c == 0 else (lane_k >= A_HEAD_DIM)
                keep.append(jnp.where(own, 1.0, 0.0).astype(BF16))
                onehot.append(jnp.where(lane_k - off == chunk_k, 1.0, 0.0).astype(BF16))
                j = lane_q - off
                hidden = jnp.where(j >= 0, jnp.where(j < n_chunks, first_chunk + j, -1), -1) > chunk_q
                penalty.append(jnp.where(hidden, MASK_VALUE, 0.0).astype(BF16))

        def scores(idx):
            hh, c = divmod(idx, 2)
            sl = slice(hh * LANES, (hh + 1) * LANES)
            k = k_ref[:, sl]
            q = (q1_ref, q2_ref)[c][:, sl]
            if masked:
                k = k * keep[c] + onehot[c]
                q = q + penalty[c]
            st_ref[idx % slots] = lax.dot_general(k, q, nt, preferred_element_type=F32)

        def accumulate(idx):
            hh = idx // 2
            st = st_ref[idx % slots]
            m_old = m_ref[idx]
            m_new = jnp.maximum(m_old, jnp.max(st, axis=0, keepdims=True))
            m_ref[idx] = m_new
            p = jnp.exp2(st - m_new).astype(BF16)
            pv = jnp.dot(vt_ref[hh * V_ROWS:(hh + 1) * V_ROWS, :], p, preferred_element_type=F32)
            acc_ref[idx] = jnp.exp2(m_old - m_new) * acc_ref[idx] + pv

        ahead = slots - 1
        for idx in range(min(ahead, n_maps)):
            scores(idx)
        for idx in range(n_maps):
            if idx + ahead < n_maps:
                scores(idx + ahead)
            accumulate(idx)

    @pl.when((flags & 2) == 0)
    def _():
        step(False)

    @pl.when((flags & 2) != 0)
    def _():
        step(True)

    @pl.when((flags & 4) != 0)
    def _():
        lam = _diff_lambda(lp_ref[...], lam_init)
        for hh in range(A_HEADS):
            outs = []
            for idx in (2 * hh, 2 * hh + 1):
                outs.append(acc_ref[idx, 0:A_V_DIM, :] * (1.0 / acc_ref[idx, A_V_DIM:A_V_DIM + 1, :]))
            o = (outs[0] - lam * outs[1]).T
            o_ref[:, hh * LANES:(hh + 1) * LANES] = _subln(o, g_ref[...], lam_init).astype(BF16)


def _attn_pairs(t, tq, tk):
    qi, kj, fl = [], [], []
    for i in range(t // tq):
        q_lo, q_hi = (i * tq) // CHUNK, (i * tq + tq - 1) // CHUNK
        js = [j for j in range(t // tk) if (j * tk) // CHUNK <= q_hi]
        for j in js:
            partial = (j * tk + tk - 1) // CHUNK > q_lo
            qi.append(i)
            kj.append(j)
            fl.append((1 if j == js[0] else 0) | (2 if partial else 0) | (4 if j == js[-1] else 0))
    return (np.asarray(qi, np.int32), np.asarray(kj, np.int32), np.asarray(fl, np.int32))


def _attn_prompt(q1, q2, kb, vt, lam_p, subln_g, layer, lam_init):
    t = q1.shape[0]
    tq, tk = min(ATTN_Q_TILE, t), min(ATTN_K_TILE, t)
    qi, kj, fl = _attn_pairs(t, tq, tk)
    grid_spec = pltpu.PrefetchScalarGridSpec(
        num_scalar_prefetch=3,
        grid=(len(qi),),
        in_specs=[
            pl.BlockSpec((tq, A_WIDTH), lambda n, qi, kj, fl: (qi[n], 0)),
            pl.BlockSpec((tq, A_WIDTH), lambda n, qi, kj, fl: (qi[n], 0)),
            pl.BlockSpec((tk, A_WIDTH), lambda n, qi, kj, fl: (kj[n], 0)),
            pl.BlockSpec((A_HEADS * V_ROWS, tk), lambda n, qi, kj, fl: (0, kj[n])),
            _pick(lam_p, layer),
            _pick(subln_g, layer),
        ],
        out_specs=pl.BlockSpec((tq, A_WIDTH), lambda n, qi, kj, fl: (qi[n], 0)),
        scratch_shapes=[pltpu.VMEM((2 * A_HEADS, 1, tq), F32),
                        pltpu.VMEM((2 * A_HEADS, V_ROWS, tq), F32),
                        pltpu.VMEM((ATTN_SCORE_SLOTS, tk, tq), F32)],
    )
    return pl.pallas_call(
        functools.partial(_attn_prompt_kernel, lam_init=lam_init),
        grid_spec=grid_spec,
        out_shape=jax.ShapeDtypeStruct((t, A_WIDTH), BF16),
        compiler_params=_params("arbitrary"),
        name="attn_prompt",
    )(jnp.asarray(qi), jnp.asarray(kj), jnp.asarray(fl), q1, q2, kb, vt, lam_p, subln_g)


def _attn_sample_kernel(q1_ref, q2_ref, kn_ref, vn_ref, ckt_ref, cv_ref, lp_ref, g_ref, o_ref, *, lam_init):
    past = cv_ref.shape[0] // A_HEADS
    tq = q1_ref.shape[0]
    qpos = past + lax.broadcasted_iota(jnp.int32, (tq, past), 0)
    vis_past = (lax.broadcasted_iota(jnp.int32, (tq, past), 1) // CHUNK) <= (qpos // CHUNK)
    qpos_n = lax.broadcasted_iota(jnp.int32, (tq, tq), 0)
    vis_new = (lax.broadcasted_iota(jnp.int32, (tq, tq), 1) + past) // CHUNK <= (qpos_n + past) // CHUNK
    lam = _diff_lambda(lp_ref[...], lam_init)
    nt = (((1,), (1,)), ((), ()))
    scores = []
    for hh in range(A_HEADS):
        sl = slice(hh * LANES, (hh + 1) * LANES)
        ckt = ckt_ref[2 * hh:2 * hh + 2].reshape(2 * A_HEAD_DIM, past).astype(BF16)
        kn = kn_ref[:, sl]
        for q_ref in (q1_ref, q2_ref):
            q = q_ref[:, sl]
            scores.append((jnp.where(vis_past, jnp.dot(q, ckt, preferred_element_type=F32), MASK_VALUE),
                           jnp.where(vis_new, lax.dot_general(q, kn, nt, preferred_element_type=F32), MASK_VALUE)))
    probs = []
    for s_p, s_n in scores:
        m = jnp.maximum(jnp.max(s_p, axis=-1, keepdims=True), jnp.max(s_n, axis=-1, keepdims=True))
        p_p = jnp.exp2(s_p - m)
        p_n = jnp.exp2(s_n - m)
        l = jnp.sum(p_p, axis=-1, keepdims=True) + jnp.sum(p_n, axis=-1, keepdims=True)
        probs.append((p_p.astype(BF16), p_n.astype(BF16), 1.0 / l))
    for hh in range(A_HEADS):
        cv = cv_ref[pl.ds(hh, past, stride=A_HEADS), :].astype(BF16)
        vn = vn_ref[pl.ds(hh, tq, stride=A_HEADS), :].astype(BF16)
        outs = []
        for p_p, p_n, inv_l in probs[2 * hh:2 * hh + 2]:
            pv = jnp.dot(p_p, cv, preferred_element_type=F32) + jnp.dot(p_n, vn, preferred_element_type=F32)
            outs.append(pv * inv_l)
        o = outs[0] - lam * outs[1]
        o_ref[:, hh * LANES:(hh + 1) * LANES] = _subln(o, g_ref[...], lam_init).astype(BF16)


def _attn_sample(q1, q2, kb, vf, cache_kt, cache_v, layer, lam_p, subln_g, lam_init):
    _, nb, nh2, hd, past = cache_kt.shape
    tq = q1.shape[0] // nb
    new = lambda: pl.BlockSpec((tq, A_WIDTH), lambda b: (b, 0))
    return pl.pallas_call(
        functools.partial(_attn_sample_kernel, lam_init=lam_init),
        grid=(nb,),
        in_specs=[new(), new(), new(), pl.BlockSpec((None, tq * A_HEADS, A_V_DIM), lambda b: (layer, b, 0)),
                  pl.BlockSpec((None, None, nh2, hd, past), lambda b: (layer, b, 0, 0, 0)),
                  pl.BlockSpec((None, None, past * A_HEADS, A_V_DIM), lambda b: (layer, b, 0, 0)),
                  _pick(lam_p, layer), _pick(subln_g, layer)],
        out_specs=new(),
        out_shape=jax.ShapeDtypeStruct((nb * tq, A_WIDTH), BF16),
        compiler_params=_params("parallel"),
        name="attn_sample",
    )(q1, q2, kb, vf, cache_kt, cache_v, lam_p, subln_g)


def _gla_kernel(q_ref, k_ref, v_ref, la_ref, r_ref, s0_ref, g_ref, o_ref, s_ref, st_ref, *, chunk, carry):
    blk = pl.program_id(1)
    tokens = q_ref.shape[0]

    if carry:
        @pl.when(blk == 0)
        def _():
            st_ref[...] = s0_ref[...].T

    span = min(tokens, max(chunk, GLA_SPAN))
    row = lax.broadcasted_iota(jnp.int32, (span, span), 0)
    col = lax.broadcasted_iota(jnp.int32, (span, span), 1)
    same_chunk = (row // chunk) == (col // chunk)
    causal = jnp.where(same_chunk, col, span) <= row
    tri = jnp.where(causal, 1.0, 0.0).astype(BF16)
    ones = jnp.where(same_chunk, 1.0, 0.0).astype(BF16)
    lane = lax.broadcasted_iota(jnp.int32, (span, B_KEYS), 1)
    nt = (((1,), (1,)), ((), ()))
    tn = (((0,), (0,)), ((), ()))

    groups = []
    for g0 in range(0, tokens, span):
        rs = slice(g0, g0 + span)
        la = la_ref[rs, :]
        la_hi = la.astype(BF16)
        la_lo = (la - la_hi.astype(F32)).astype(BF16)
        b = (jnp.dot(tri, la_hi, preferred_element_type=F32)
             + jnp.dot(tri, la_lo, preferred_element_type=F32))
        total = (jnp.dot(ones, la_hi, preferred_element_type=F32)
                 + jnp.dot(ones, la_lo, preferred_element_type=F32))
        qe = q_ref[rs, :] * jnp.exp(b)
        k = k_ref[rs, :]
        ke = (k * jnp.exp(-b)).astype(BF16)
        kd = k * jnp.exp(total - b)
        qe_h, kd_h, o_intra = [], [], []
        for hh in range(B_HEADS):
            in_head = (lane // B_KEY_DIM) == hh
            qe_h.append(jnp.where(in_head, qe, 0.0).astype(BF16))
            kd_h.append(jnp.where(in_head, kd, 0.0).astype(BF16))
            att = lax.dot_general(qe_h[hh], ke, nt, preferred_element_type=F32)
            att = jnp.where(causal, att, 0.0).astype(BF16)
            o_intra.append(jnp.dot(att, v_ref[rs, hh * B_VAL_DIM:(hh + 1) * B_VAL_DIM], preferred_element_type=F32))
        groups.append((qe_h, kd_h, o_intra, total))

    n_chunks = tokens // chunk
    where = [(groups[c * chunk // span], slice(c * chunk % span, c * chunk % span + chunk)) for c in range(n_chunks)]
    increments = []
    for c, ((_, kd_h, _, _), rl) in enumerate(where):
        r = slice(c * chunk, (c + 1) * chunk)
        kd_stack = jnp.concatenate([kd_h[hh][rl] for hh in range(B_HEADS)], axis=0)
        v_stack = jnp.concatenate([v_ref[r, hh * B_VAL_DIM:(hh + 1) * B_VAL_DIM] for hh in range(B_HEADS)], axis=0)
        increments.append(lax.dot_general(v_stack, kd_stack, tn, preferred_element_type=F32))
    states = []
    st = st_ref[...] if carry else None
    for c, ((_, _, _, total), rl) in enumerate(where):
        if not carry:
            st = s0_ref[c].T
        states.append(st)
        st = st * jnp.exp(total[rl.start:rl.start + 1, :]) + increments[c]
        if not carry:
            s_ref[c] = st.T
    if carry:
        st_ref[...] = st
    for c, ((qe_h, _, o_intra, _), rl) in enumerate(where):
        r = slice(c * chunk, (c + 1) * chunk)
        q_stack = jnp.concatenate([qe_h[hh][rl] for hh in range(B_HEADS)], axis=0)
        o_inter = lax.dot_general(q_stack, states[c].astype(BF16), nt, preferred_element_type=F32)
        for hh in range(B_HEADS):
            vs = slice(hh * B_VAL_DIM, (hh + 1) * B_VAL_DIM)
            o = o_intra[hh][rl] + o_inter[hh * chunk:(hh + 1) * chunk]
            rg = r_ref[r, vs]
            o_ref[r, vs] = (_rms(o, g_ref[...], SUBLN_EPS) * (rg * jax.nn.sigmoid(rg))).astype(BF16)

    if carry:
        @pl.when(blk == pl.num_programs(1) - 1)
        def _():
            s_ref[...] = st_ref[...].T


def _gla(q, k, v, la, rg, s0, s0_layer, g, layer, chunk):
    nb = s0.shape[1]
    t = q.shape[0] // nb
    carry = t > chunk
    if carry:
        tb = min(GLA_BLOCK, t)
        grid = (nb, t // tb)
        state = pl.BlockSpec((None, B_KEYS, B_VAL_DIM), lambda b, i: (b, 0, 0))
        state_in = pl.BlockSpec((None, None, B_KEYS, B_VAL_DIM), lambda b, i: (s0_layer, b, 0, 0))
    else:
        assert t == chunk
        streams = math.gcd(nb, max(1, GLA_BLOCK // t))
        tb = streams * t
        grid = (nb // streams, 1)
        state = pl.BlockSpec((streams, B_KEYS, B_VAL_DIM), lambda b, i: (b, 0, 0))
        state_in = pl.BlockSpec((None, streams, B_KEYS, B_VAL_DIM), lambda b, i: (s0_layer, b, 0, 0))
    nblk = grid[1]
    tok = lambda n: pl.BlockSpec((tb, n), lambda b, i: (b * nblk + i, 0))
    return pl.pallas_call(
        functools.partial(_gla_kernel, chunk=chunk, carry=carry),
        grid=grid,
        in_specs=[tok(B_KEYS), tok(B_KEYS), tok(B_WIDTH), tok(B_KEYS), tok(B_WIDTH), state_in, _pick(g, layer)],
        out_specs=[tok(B_WIDTH), state],
        out_shape=[jax.ShapeDtypeStruct((nb * t, B_WIDTH), BF16),
                   jax.ShapeDtypeStruct((nb, B_KEYS, B_VAL_DIM), F32)],
        scratch_shapes=[pltpu.VMEM((B_VAL_DIM, B_KEYS), F32)],
        compiler_params=_params("parallel", "arbitrary"),
        name="gla",
    )(q, k, v, la, rg, s0, g)


def _merge_kernel(x_ref, oa_ref, ob_ref, gpre_ref, gpost_ref, wgab_ref, woa_ref, wob_ref, wo_ref, o_ref):
    tm, d = x_ref.shape
    halves = [slice(i * (tm // MERGE_SPLIT), (i + 1) * (tm // MERGE_SPLIT)) for i in range(MERGE_SPLIT)]
    nt = (((1,), (1,)), ((), ()))
    first = []
    for r in halves:
        h = _rms(x_ref[r, :], gpre_ref[...], NORM_EPS).astype(BF16)
        first.append((lax.dot_general(h, wgab_ref[0], nt, preferred_element_type=F32),
                      jnp.dot(oa_ref[r, :], woa_ref[...], preferred_element_type=F32),
                      jnp.dot(ob_ref[r, :], wob_ref[...], preferred_element_type=F32)))
    mixed = []
    for gab, ya, yb in first:
        m = jax.nn.sigmoid(gab[:, :d]) * ya + jax.nn.sigmoid(gab[:, d:]) * yb
        mixed.append(jnp.dot(m.astype(BF16), wo_ref[...], preferred_element_type=F32))
    for r, y in zip(halves, mixed):
        o_ref[r, :] = x_ref[r, :] + _rms(y, gpost_ref[...], NORM_EPS)


def _merge(x, oa, ob, norm, w_t, woa, wob, wo, layer):
    t, d = x.shape
    tm = min(MERGE_ROW_TILE, t)
    row = lambda n: pl.BlockSpec((tm, n), lambda i: (i, 0))
    gate_rows = w_t.shape[1] - 2 * d
    w_gates = pl.BlockSpec((pl.Element(1), pl.Element(2 * d), pl.Element(d)), lambda i: (layer, gate_rows, 0),
                           pipeline_mode=pl.Buffered(1))
    return pl.pallas_call(
        _merge_kernel,
        grid=(t // tm,),
        in_specs=[row(d), row(A_WIDTH), row(B_WIDTH), _pick(norm, layer, 2), _pick(norm, layer, 3),
                  w_gates, _pick(woa, layer), _pick(wob, layer), _pick(wo, layer)],
        out_specs=row(d),
        out_shape=jax.ShapeDtypeStruct((t, d), F32),
        compiler_params=_params("parallel"),
        name="mixer_merge",
    )(x, oa, ob, norm, norm, w_t, woa, wob, wo)


def _rope_tables(pos):
    half = A_HEAD_DIM // 2
    inv = jnp.exp(-math.log(ROPE_THETA) * jnp.arange(half, dtype=F32) / half)
    ang = pos.astype(F32)[:, None] * inv[None, :]
    cos, sin = jnp.cos(ang), jnp.sin(ang)
    reps = LANES // A_HEAD_DIM
    return (jnp.tile(jnp.concatenate([cos, cos], axis=-1), (1, reps)),
            jnp.tile(jnp.concatenate([-sin, sin], axis=-1), (1, reps)))


def kernel(x_prompt, x_sample, cache_k, cache_v, state_gla, norm_g, ffn_w_gate, ffn_w_up, ffn_w_down, w_in,
           w_gate_up, b_gate, lambda_p, subln_g, gla_norm_g, w_out_a, w_out_b, w_out):
    depth = w_in.shape[0]
    bp, tp, d = x_prompt.shape
    bs, ts, _ = x_sample.shape
    past = cache_k.shape[2]
    assert bp == 1 and tp % CHUNK == 0

    n_qk = 2 * A_HEADS * A_HEAD_DIM
    assert w_in.shape[2] == 2 * n_qk + A_WIDTH + 2 * B_KEYS + 2 * B_WIDTH + GATE_RANK + 2 * d

    norm = norm_g.reshape(depth, norm_g.shape[1], 1, d)
    wg, wu, wd = ffn_w_gate.astype(BF16), ffn_w_up.astype(BF16), ffn_w_down.astype(BF16)
    w_t = jnp.transpose(w_in, (0, 2, 1)).astype(BF16)
    w_gu = w_gate_up.astype(BF16)
    bg = b_gate[:, None, :]
    woa, wob, wo = w_out_a.astype(BF16), w_out_b.astype(BF16), w_out.astype(BF16)
    sg = subln_g[:, None, :]
    gg = gla_norm_g[:, None, :]

    cos_p, sin_p = _rope_tables(jnp.arange(tp))
    cos_s, sin_s = _rope_tables(jnp.tile(past + jnp.arange(ts), bs))

    cache_kt = jnp.transpose(cache_k, (0, 1, 3, 4, 2))
    cache_vr = cache_v.reshape(depth, bs, past * A_HEADS, A_V_DIM)
    s0_sample = state_gla.reshape(depth, bs, B_KEYS, B_VAL_DIM)
    s0_prompt = jnp.zeros((1, bp, B_KEYS, B_VAL_DIM), F32)

    xp = x_prompt.reshape(tp, d)
    xs = x_sample.reshape(bs * ts, d)
    kv_p = kv_s = None
    states_p, states_s = [], []
    for l in range(depth):
        lam_init = 0.8 - 0.6 * math.exp(-0.3 * l)

        def layer(x, cos, sin, chunk, s0, s0_layer, kv, is_prompt):
            x = _ffn(x, norm, wg, wu, wd, l, 0)
            kf, vf, q1, q2, kb, vt, qg, kg, vg, rg, la = _proj(x, norm, w_t, w_gu, bg, cos, sin, l, is_prompt, kv)
            if is_prompt:
                oa = _attn_prompt(q1, q2, kb, vt, lambda_p, sg, l, lam_init)
            else:
                oa = _attn_sample(q1, q2, kb, vf, cache_kt, cache_vr, l, lambda_p, sg, lam_init)
            ob, s_new = _gla(qg, kg, vg, la, rg, s0, s0_layer, gg, l, chunk)
            x = _merge(x, oa, ob, norm, w_t, woa, wob, wo, l)
            x = _ffn(x, norm, wg, wu, wd, l, 1)
            return x, (kf, vf), s_new

        xp, kv_p, sp = layer(xp, cos_p, sin_p, CHUNK, s0_prompt, 0, kv_p, True)
        xs, kv_s, sn = layer(xs, cos_s, sin_s, ts, s0_sample, l, kv_s, False)
        states_p.append(sp)
        states_s.append(sn)

    k_prompt = jnp.transpose(kv_p[0].reshape(depth, bp, 2 * A_HEADS, A_HEAD_DIM, tp), (0, 1, 4, 2, 3))
    v_prompt = kv_p[1].reshape(depth, bp, tp, A_HEADS, A_V_DIM)
    s_prompt = jnp.stack(states_p).reshape(depth, bp, B_HEADS, B_KEY_DIM, B_VAL_DIM)
    k_sample = kv_s[0].reshape(depth, bs, ts, 2 * A_HEADS, A_HEAD_DIM)
    v_sample = kv_s[1].reshape(depth, bs, ts, A_HEADS, A_V_DIM)
    s_sample = jnp.stack(states_s).reshape(depth, bs, B_HEADS, B_KEY_DIM, B_VAL_DIM)
    return (xp.reshape(bp, tp, d), xs.reshape(bs, ts, d), k_prompt, v_prompt, s_prompt,
            k_sample, v_sample, s_sample)
```

```python
import functools
import math

import jax
import jax.numpy as jnp
import numpy as np
from jax import lax
from jax.experimental import pallas as pl
from jax.experimental.pallas import tpu as pltpu

F32 = jnp.float32
BF16 = jnp.bfloat16

CHUNK = 64
A_HEADS = 4
A_HEAD_DIM = 64
A_V_DIM = 2 * A_HEAD_DIM
A_WIDTH = A_HEADS * A_V_DIM
B_HEADS = 4
B_KEY_DIM = 64
B_VAL_DIM = 128
B_KEYS = B_HEADS * B_KEY_DIM
B_WIDTH = B_HEADS * B_VAL_DIM
GATE_RANK = 16
GATE_TAU = 16.0
ROPE_THETA = 10000.0
NORM_EPS = 1e-6
SUBLN_EPS = 1e-5
LANES = 128
MASK_VALUE = -1e30

ROW_TILE = 512
FFN_ROW_TILE = 1024
ROW_GROUP = 256
MERGE_ROW_TILE = 1024
MERGE_SPLIT = 2
ATTN_Q_TILE = 512
ATTN_K_TILE = 1024
ATTN_K_GROUP = 2
ATTN_SCORE_SLOTS = 3
V_ROWS = A_V_DIM + 16
LOG2E = math.log2(math.e)
GLA_BLOCK = 1024
GLA_SPAN = 256
VMEM_LIMIT = 56 * 1024 * 1024


def _rms(x, g, eps):
    return x * lax.rsqrt(jnp.mean(x * x, axis=-1, keepdims=True) + eps) * g


def _params(*sem):
    return pltpu.CompilerParams(dimension_semantics=sem, vmem_limit_bytes=VMEM_LIMIT)


def _row_groups(rows, group):
    return [slice(i, i + group) for i in range(0, rows, group)]


def _ffn_kernel(x_ref, gpre_ref, gpost_ref, wg_ref, wu_ref, wd_ref, o_ref):
    groups = _row_groups(x_ref.shape[0], min(x_ref.shape[0], ROW_GROUP))
    gate_up = []
    for r in groups:
        h = _rms(x_ref[r, :], gpre_ref[...], NORM_EPS).astype(BF16)
        gate_up.append((jnp.dot(h, wg_ref[...], preferred_element_type=F32),
                        jnp.dot(h, wu_ref[...], preferred_element_type=F32)))
    down = []
    for g, u in gate_up:
        a = (g * jax.nn.sigmoid(g) * u).astype(BF16)
        down.append(jnp.dot(a, wd_ref[...], preferred_element_type=F32))
    for r, y in zip(groups, down):
        o_ref[r, :] = x_ref[r, :] + 0.5 * _rms(y, gpost_ref[...], NORM_EPS)


def _pick(a, *lead):
    rest = a.shape[len(lead):]
    index = tuple(lead) + (0,) * len(rest)
    return pl.BlockSpec((None,) * len(lead) + rest, lambda *_: index, pipeline_mode=pl.Buffered(1))


def _ffn(x, norm, wg, wu, wd, layer, which):
    t, d = x.shape
    tm = min(FFN_ROW_TILE, t)
    row = pl.BlockSpec((tm, d), lambda i: (i, 0))
    return pl.pallas_call(
        _ffn_kernel,
        grid=(t // tm,),
        in_specs=[row, _pick(norm, layer, 4 * which), _pick(norm, layer, 4 * which + 1),
                  _pick(wg, layer, which), _pick(wu, layer, which), _pick(wd, layer, which)],
        out_specs=row,
        out_shape=jax.ShapeDtypeStruct((t, d), F32),
        compiler_params=_params("parallel"),
        name="ffn",
    )(x, norm, norm, wg, wu, wd)


def _rope(z, cos, sin_signed, first_half):
    rot = jnp.where(first_half, pltpu.roll(z, LANES - A_HEAD_DIM // 2, 1), pltpu.roll(z, A_HEAD_DIM // 2, 1))
    return z * cos + rot * sin_signed


def _proj_kernel(x_ref, g_ref, w_ref, wc_ref, wgu_ref, bg_ref, cos_ref, sin_ref, *rest, k_transposed, layer, n_alias):
    (kf_ref, vf_ref, q1_ref, q2_ref, kb_ref, vt_ref, qg_ref, kg_ref, vg_ref, rg_ref, la_ref) = rest[n_alias:]
    nt = (((1,), (1,)), ((), ()))
    if n_alias == 0:
        for j in range(kf_ref.shape[0]):
            if j != layer:
                kf_ref[j] = jnp.zeros(kf_ref.shape[1:], F32)
                vf_ref[j] = jnp.zeros(vf_ref.shape[1:], F32)
        kf_ref, vf_ref = kf_ref.at[layer], vf_ref.at[layer]
    h = _rms(x_ref[...], g_ref[...], NORM_EPS).astype(BF16)
    z = lax.dot_general(h, w_ref[...], nt, preferred_element_type=F32)
    cos = cos_ref[...]
    sin = sin_ref[...]
    lane = lax.broadcasted_iota(jnp.int32, cos.shape, 1)
    first_half = (lane % A_HEAD_DIM) < (A_HEAD_DIM // 2)
    even_head = lane < A_HEAD_DIM
    scale = A_HEAD_DIM ** -0.5 * LOG2E
    o = 0
    for hh in range(A_HEADS):
        sl = slice(hh * LANES, (hh + 1) * LANES)
        q = _rope(z[:, o + hh * LANES:o + (hh + 1) * LANES], cos, sin, first_half) * scale
        q1_ref[:, sl] = jnp.where(even_head, q, 0.0).astype(BF16)
        q2_ref[:, sl] = jnp.where(even_head, 0.0, q).astype(BF16)
    o += 2 * A_HEADS * A_HEAD_DIM
    for hh in range(A_HEADS):
        sl = slice(hh * LANES, (hh + 1) * LANES)
        k = _rope(z[:, o + hh * LANES:o + (hh + 1) * LANES], cos, sin, first_half)
        if k_transposed:
            kf_ref[sl, :] = k.T
        else:
            kf_ref[:, sl] = k
        kb_ref[:, sl] = k.astype(BF16)
    o += 2 * A_HEADS * A_HEAD_DIM
    va = z[:, o:o + A_WIDTH]
    for hh in range(A_HEADS):
        vf_ref[pl.ds(hh, va.shape[0], stride=A_HEADS), :] = va[:, hh * A_V_DIM:(hh + 1) * A_V_DIM]
        vt_ref[hh * V_ROWS:hh * V_ROWS + A_V_DIM, :] = va[:, hh * A_V_DIM:(hh + 1) * A_V_DIM].T.astype(BF16)
        vt_ref[hh * V_ROWS + A_V_DIM:(hh + 1) * V_ROWS, :] = jnp.ones((V_ROWS - A_V_DIM, va.shape[0]), BF16)
    o += A_WIDTH
    qg_ref[...] = z[:, o:o + B_KEYS] * (B_KEY_DIM ** -0.5)
    o += B_KEYS
    kg_ref[...] = z[:, o:o + B_KEYS]
    o += B_KEYS
    vg_ref[...] = z[:, o:o + B_WIDTH].astype(BF16)
    o += B_WIDTH
    rg_ref[...] = z[:, o:o + B_WIDTH]
    code = lax.dot_general(h, wc_ref[...], nt, preferred_element_type=F32).astype(BF16)
    gate = jnp.dot(code, wgu_ref[...], preferred_element_type=F32) + bg_ref[...]
    la_ref[...] = (jnp.minimum(gate, 0.0) - jnp.log1p(jnp.exp(-jnp.abs(gate)))) * (1.0 / GATE_TAU)


def _proj(x, norm, w_t, wgu, bg, cos, sin, layer, k_transposed, kv_prev):
    t, d = x.shape
    depth = w_t.shape[0]
    tm = min(ROW_TILE, t)
    row = lambda n: pl.BlockSpec((tm, n), lambda i: (i, 0))
    col = lambda n: pl.BlockSpec((n, tm), lambda i: (0, i))
    sds = jax.ShapeDtypeStruct
    n_qk = 2 * A_HEADS * A_HEAD_DIM
    n_main = 2 * n_qk + A_WIDTH + 2 * B_KEYS + 2 * B_WIDTH
    assert n_main % GATE_RANK == 0
    k_shape = (n_qk, t) if k_transposed else (t, n_qk)
    k_block = (n_qk, tm) if k_transposed else (tm, n_qk)
    k_index = (lambda i: (0, i)) if k_transposed else (lambda i: (i, 0))
    v_shape, v_block = (t * A_HEADS, A_V_DIM), (tm * A_HEADS, A_V_DIM)
    if kv_prev is None:
        kv_specs = [pl.BlockSpec((depth,) + k_block, lambda i: (0,) + k_index(i)),
                    pl.BlockSpec((depth,) + v_block, lambda i: (0, i, 0))]
        alias_specs, alias_args, aliases = [], (), {}
    else:
        kv_specs = [pl.BlockSpec((None,) + k_block, lambda i: (layer,) + k_index(i)),
                    pl.BlockSpec((None,) + v_block, lambda i: (layer, i, 0))]
        alias_specs = [pl.BlockSpec(memory_space=pl.ANY)] * 2
        alias_args, aliases = tuple(kv_prev), {8: 0, 9: 1}
    return pl.pallas_call(
        functools.partial(_proj_kernel, k_transposed=k_transposed, layer=layer, n_alias=len(alias_args)),
        grid=(t // tm,),
        in_specs=[row(d), _pick(norm, layer, 2),
                  pl.BlockSpec((None, n_main, d), lambda i: (layer, 0, 0), pipeline_mode=pl.Buffered(1)),
                  pl.BlockSpec((None, GATE_RANK, d), lambda i: (layer, n_main // GATE_RANK, 0),
                               pipeline_mode=pl.Buffered(1)),
                  _pick(wgu, layer), _pick(bg, layer), row(LANES), row(LANES)] + alias_specs,
        out_specs=kv_specs + [row(A_WIDTH), row(A_WIDTH), row(A_WIDTH), col(A_HEADS * V_ROWS),
                              row(B_KEYS), row(B_KEYS), row(B_WIDTH), row(B_WIDTH), row(B_KEYS)],
        out_shape=[sds((depth,) + k_shape, F32), sds((depth,) + v_shape, F32),
                   sds((t, A_WIDTH), BF16), sds((t, A_WIDTH), BF16), sds((t, A_WIDTH), BF16),
                   sds((A_HEADS * V_ROWS, t), BF16),
                   sds((t, B_KEYS), F32), sds((t, B_KEYS), F32), sds((t, B_WIDTH), BF16),
                   sds((t, B_WIDTH), F32), sds((t, B_KEYS), F32)],
        input_output_aliases=aliases,
        compiler_params=_params("parallel"),
        name="mixer_proj",
    )(x, norm, w_t, w_t, wgu, bg, cos, sin, *alias_args)


def _diff_lambda(lp, lam_init):
    a = jnp.sum(lp[0:1] * lp[1:2], axis=-1, keepdims=True)
    b = jnp.sum(lp[2:3] * lp[3:4], axis=-1, keepdims=True)
    return jnp.exp(a) - jnp.exp(b) + lam_init


def _subln(o, g, lam_init):
    return _rms(o, g, SUBLN_EPS) * (1.0 - lam_init)


def _attn_prompt_kernel(qi_ref, kg_ref, fl_ref, q1_ref, q2_ref, k_ref, vt_ref, lp_ref, g_ref,
                        o_ref, m_ref, acc_ref, st_ref, *, lam_init, variants):
    n = pl.program_id(0)
    slots, tk, tq = st_ref.shape
    group = k_ref.shape[0] // tk
    flags = fl_ref[n]
    n_maps = 2 * A_HEADS
    nt = (((1,), (1,)), ((), ()))

    @pl.when((flags & 1) != 0)
    def _():
        m_ref[...] = jnp.full(m_ref.shape, MASK_VALUE, F32)
        acc_ref[...] = jnp.zeros(acc_ref.shape, F32)

    def step(n_present, n_masked):
        penalty = {}
        if n_masked:
            n_chunks = tk // CHUNK
            assert n_chunks <= A_HEAD_DIM
            lane_k = lax.broadcasted_iota(jnp.int32, (tk, LANES), 1)
            chunk_k = lax.broadcasted_iota(jnp.int32, (tk, LANES), 0) // CHUNK
            lane_q = lax.broadcasted_iota(jnp.int32, (tq, LANES), 1)
            chunk_q = (qi_ref[n] * tq) // CHUNK + lax.broadcasted_iota(jnp.int32, (tq, LANES), 0) // CHUNK
            keep, onehot = [], []
            for c in range(2):
                off = A_HEAD_DIM * (1 - c)
                own = (lane_k < A_HEAD_DIM) if c == 0 else (lane_k >= A_HEAD_DIM)
                keep.append(jnp.where(own, 1.0, 0.0).astype(BF16))
                onehot.append(jnp.where(lane_k - off == chunk_k, 1.0, 0.0).astype(BF16))
                j = lane_q - off
                for s in range(n_present - n_masked, n_present):
                    first_chunk = ((kg_ref[n] * group + s) * tk) // CHUNK
                    hidden = jnp.where(j >= 0, jnp.where(j < n_chunks, first_chunk + j, -1), -1) > chunk_q
                    penalty[s, c] = jnp.where(hidden, MASK_VALUE, 0.0).astype(BF16)

        units = [(s, idx) for s in range(n_present) for idx in range(n_maps)]

        def scores(u):
            s, idx = units[u]
            hh, c = divmod(idx, 2)
            sl = slice(hh * LANES, (hh + 1) * LANES)
            k = k_ref[s * tk:(s + 1) * tk, sl]
            q = (q1_ref, q2_ref)[c][:, sl]
            if (s, c) in penalty:
                k = k * keep[c] + onehot[c]
                q = q + penalty[s, c]
            st_ref[u % slots] = lax.dot_general(k, q, nt, preferred_element_type=F32)

        def accumulate(u):
            s, idx = units[u]
            hh = idx // 2
            st = st_ref[u % slots]
            m_old = m_ref[idx]
            m_new = jnp.maximum(m_old, jnp.max(st, axis=0, keepdims=True))
            m_ref[idx] = m_new
            p = jnp.exp2(st - m_new).astype(BF16)
            pv = jnp.dot(vt_ref[hh * V_ROWS:(hh + 1) * V_ROWS, s * tk:(s + 1) * tk], p, preferred_element_type=F32)
            acc_ref[idx] = jnp.exp2(m_old - m_new) * acc_ref[idx] + pv

        ahead = slots - 1
        for u in range(min(ahead, len(units))):
            scores(u)
        for u in range(len(units)):
            if u + ahead < len(units):
                scores(u + ahead)
            accumulate(u)

    for v, (n_present, n_masked) in enumerate(variants):
        @pl.when((flags >> 2) == v)
        def _():
            step(n_present, n_masked)

    @pl.when((flags & 2) != 0)
    def _():
        lam = _diff_lambda(lp_ref[...], lam_init)
        for hh in range(A_HEADS):
            outs = []
            for idx in (2 * hh, 2 * hh + 1):
                outs.append(acc_ref[idx, 0:A_V_DIM, :] * (1.0 / acc_ref[idx, A_V_DIM:A_V_DIM + 1, :]))
            o = (outs[0] - lam * outs[1]).T
            o_ref[:, hh * LANES:(hh + 1) * LANES] = _subln(o, g_ref[...], lam_init).astype(BF16)


def _attn_steps(t, tq, tk, group):
    qi, kg, fl, variants = [], [], [], []
    for i in range(t // tq):
        q_lo, q_hi = (i * tq) // CHUNK, (i * tq + tq - 1) // CHUNK
        js = [j for j in range(t // tk) if (j * tk) // CHUNK <= q_hi]
        for g in range(0, len(js), group):
            members = js[g:g + group]
            variant = (len(members), sum((j * tk + tk - 1) // CHUNK > q_lo for j in members))
            if variant not in variants:
                variants.append(variant)
            qi.append(i)
            kg.append(g // group)
            fl.append((1 if g == 0 else 0) | (2 if g + group >= len(js) else 0) | (variants.index(variant) << 2))
    return (np.asarray(qi, np.int32), np.asarray(kg, np.int32), np.asarray(fl, np.int32)), tuple(variants)


def _attn_prompt(q1, q2, kb, vt, lam_p, subln_g, layer, lam_init):
    t = q1.shape[0]
    tq, tk = min(ATTN_Q_TILE, t), min(ATTN_K_TILE, t)
    group = ATTN_K_GROUP if t % (ATTN_K_GROUP * tk) == 0 else 1
    (qi, kg, fl), variants = _attn_steps(t, tq, tk, group)
    grid_spec = pltpu.PrefetchScalarGridSpec(
        num_scalar_prefetch=3,
        grid=(len(qi),),
        in_specs=[
            pl.BlockSpec((tq, A_WIDTH), lambda n, qi, kg, fl: (qi[n], 0)),
            pl.BlockSpec((tq, A_WIDTH), lambda n, qi, kg, fl: (qi[n], 0)),
            pl.BlockSpec((group * tk, A_WIDTH), lambda n, qi, kg, fl: (kg[n], 0)),
            pl.BlockSpec((A_HEADS * V_ROWS, group * tk), lambda n, qi, kg, fl: (0, kg[n])),
            _pick(lam_p, layer),
            _pick(subln_g, layer),
        ],
        out_specs=pl.BlockSpec((tq, A_WIDTH), lambda n, qi, kg, fl: (qi[n], 0)),
        scratch_shapes=[pltpu.VMEM((2 * A_HEADS, 1, tq), F32),
                        pltpu.VMEM((2 * A_HEADS, V_ROWS, tq), F32),
                        pltpu.VMEM((ATTN_SCORE_SLOTS, tk, tq), F32)],
    )
    return pl.pallas_call(
        functools.partial(_attn_prompt_kernel, lam_init=lam_init, variants=variants),
        grid_spec=grid_spec,
        out_shape=jax.ShapeDtypeStruct((t, A_WIDTH), BF16),
        compiler_params=_params("arbitrary"),
        name="attn_prompt",
    )(jnp.asarray(qi), jnp.asarray(kg), jnp.asarray(fl), q1, q2, kb, vt, lam_p, subln_g)


def _attn_sample_kernel(q1_ref, q2_ref, kn_ref, vn_ref, ckt_ref, cv_ref, lp_ref, g_ref, o_ref, *, lam_init):
    past = cv_ref.shape[0] // A_HEADS
    tq = q1_ref.shape[0]
    qpos = past + lax.broadcasted_iota(jnp.int32, (tq, past), 0)
    vis_past = (lax.broadcasted_iota(jnp.int32, (tq, past), 1) // CHUNK) <= (qpos // CHUNK)
    qpos_n = lax.broadcasted_iota(jnp.int32, (tq, tq), 0)
    vis_new = (lax.broadcasted_iota(jnp.int32, (tq, tq), 1) + past) // CHUNK <= (qpos_n + past) // CHUNK
    lam = _diff_lambda(lp_ref[...], lam_init)
    nt = (((1,), (1,)), ((), ()))
    scores = []
    for hh in range(A_HEADS):
        sl = slice(hh * LANES, (hh + 1) * LANES)
        ckt = ckt_ref[2 * hh:2 * hh + 2].reshape(2 * A_HEAD_DIM, past).astype(BF16)
        kn = kn_ref[:, sl]
        for q_ref in (q1_ref, q2_ref):
            q = q_ref[:, sl]
            scores.append((jnp.where(vis_past, jnp.dot(q, ckt, preferred_element_type=F32), MASK_VALUE),
                           jnp.where(vis_new, lax.dot_general(q, kn, nt, preferred_element_type=F32), MASK_VALUE)))
    probs = []
    for s_p, s_n in scores:
        m = jnp.maximum(jnp.max(s_p, axis=-1, keepdims=True), jnp.max(s_n, axis=-1, keepdims=True))
        p_p = jnp.exp2(s_p - m)
        p_n = jnp.exp2(s_n - m)
        l = jnp.sum(p_p, axis=-1, keepdims=True) + jnp.sum(p_n, axis=-1, keepdims=True)
        probs.append((p_p.astype(BF16), p_n.astype(BF16), 1.0 / l))
    for hh in range(A_HEADS):
        cv = cv_ref[pl.ds(hh, past, stride=A_HEADS), :].astype(BF16)
        vn = vn_ref[pl.ds(hh, tq, stride=A_HEADS), :].astype(BF16)
        outs = []
        for p_p, p_n, inv_l in probs[2 * hh:2 * hh + 2]:
            pv = jnp.dot(p_p, cv, preferred_element_type=F32) + jnp.dot(p_n, vn, preferred_element_type=F32)
            outs.append(pv * inv_l)
        o = outs[0] - lam * outs[1]
        o_ref[:, hh * LANES:(hh + 1) * LANES] = _subln(o, g_ref[...], lam_init).astype(BF16)


def _attn_sample(q1, q2, kb, vf, cache_kt, cache_v, layer, lam_p, subln_g, lam_init):
    _, nb, nh2, hd, past = cache_kt.shape
    tq = q1.shape[0] // nb
    new = lambda: pl.BlockSpec((tq, A_WIDTH), lambda b: (b, 0))
    return pl.pallas_call(
        functools.partial(_attn_sample_kernel, lam_init=lam_init),
        grid=(nb,),
        in_specs=[new(), new(), new(), pl.BlockSpec((None, tq * A_HEADS, A_V_DIM), lambda b: (layer, b, 0)),
                  pl.BlockSpec((None, None, nh2, hd, past), lambda b: (layer, b, 0, 0, 0)),
                  pl.BlockSpec((None, None, past * A_HEADS, A_V_DIM), lambda b: (layer, b, 0, 0)),
                  _pick(lam_p, layer), _pick(subln_g, layer)],
        out_specs=new(),
        out_shape=jax.ShapeDtypeStruct((nb * tq, A_WIDTH), BF16),
        compiler_params=_params("parallel"),
        name="attn_sample",
    )(q1, q2, kb, vf, cache_kt, cache_v, lam_p, subln_g)


def _gla_kernel(q_ref, k_ref, v_ref, la_ref, r_ref, s0_ref, g_ref, o_ref, s_ref, st_ref, *, chunk, carry):
    blk = pl.program_id(1)
    tokens = q_ref.shape[0]

    if carry:
        @pl.when(blk == 0)
        def _():
            st_ref[...] = s0_ref[...].T

    span = min(tokens, max(chunk, GLA_SPAN))
    row = lax.broadcasted_iota(jnp.int32, (span, span), 0)
    col = lax.broadcasted_iota(jnp.int32, (span, span), 1)
    same_chunk = (row // chunk) == (col // chunk)
    causal = jnp.where(same_chunk, col, span) <= row
    tri = jnp.where(causal, 1.0, 0.0).astype(BF16)
    ones = jnp.where(same_chunk, 1.0, 0.0).astype(BF16)
    lane = lax.broadcasted_iota(jnp.int32, (span, B_KEYS), 1)
    nt = (((1,), (1,)), ((), ()))
    tn = (((0,), (0,)), ((), ()))

    groups = []
    for g0 in range(0, tokens, span):
        rs = slice(g0, g0 + span)
        la = la_ref[rs, :]
        la_hi = la.astype(BF16)
        la_lo = (la - la_hi.astype(F32)).astype(BF16)
        b = (jnp.dot(tri, la_hi, preferred_element_type=F32)
             + jnp.dot(tri, la_lo, preferred_element_type=F32))
        total = (jnp.dot(ones, la_hi, preferred_element_type=F32)
                 + jnp.dot(ones, la_lo, preferred_element_type=F32))
        qe = q_ref[rs, :] * jnp.exp(b)
        k = k_ref[rs, :]
        ke = (k * jnp.exp(-b)).astype(BF16)
        kd = k * jnp.exp(total - b)
        qe_h, kd_h, o_intra = [], [], []
        for hh in range(B_HEADS):
            in_head = (lane // B_KEY_DIM) == hh
            qe_h.append(jnp.where(in_head, qe, 0.0).astype(BF16))
            kd_h.append(jnp.where(in_head, kd, 0.0).astype(BF16))
            att = lax.dot_general(qe_h[hh], ke, nt, preferred_element_type=F32)
            att = jnp.where(causal, att, 0.0).astype(BF16)
            o_intra.append(jnp.dot(att, v_ref[rs, hh * B_VAL_DIM:(hh + 1) * B_VAL_DIM], preferred_element_type=F32))
        groups.append((qe_h, kd_h, o_intra, total))

    n_chunks = tokens // chunk
    where = [(groups[c * chunk // span], slice(c * chunk % span, c * chunk % span + chunk)) for c in range(n_chunks)]
    increments = []
    for c, ((_, kd_h, _, _), rl) in enumerate(where):
        r = slice(c * chunk, (c + 1) * chunk)
        kd_stack = jnp.concatenate([kd_h[hh][rl] for hh in range(B_HEADS)], axis=0)
        v_stack = jnp.concatenate([v_ref[r, hh * B_VAL_DIM:(hh + 1) * B_VAL_DIM] for hh in range(B_HEADS)], axis=0)
        increments.append(lax.dot_general(v_stack, kd_stack, tn, preferred_element_type=F32))
    states = []
    st = st_ref[...] if carry else None
    for c, ((_, _, _, total), rl) in enumerate(where):
        if not carry:
            st = s0_ref[c].T
        states.append(st)
        st = st * jnp.exp(total[rl.start:rl.start + 1, :]) + increments[c]
        if not carry:
            s_ref[c] = st.T
    if carry:
        st_ref[...] = st
    for c, ((qe_h, _, o_intra, _), rl) in enumerate(where):
        r = slice(c * chunk, (c + 1) * chunk)
        q_stack = jnp.concatenate([qe_h[hh][rl] for hh in range(B_HEADS)], axis=0)
        o_inter = lax.dot_general(q_stack, states[c].astype(BF16), nt, preferred_element_type=F32)
        for hh in range(B_HEADS):
            vs = slice(hh * B_VAL_DIM, (hh + 1) * B_VAL_DIM)
            o = o_intra[hh][rl] + o_inter[hh * chunk:(hh + 1) * chunk]
            rg = r_ref[r, vs]
            o_ref[r, vs] = (_rms(o, g_ref[...], SUBLN_EPS) * (rg * jax.nn.sigmoid(rg))).astype(BF16)

    if carry:
        @pl.when(blk == pl.num_programs(1) - 1)
        def _():
            s_ref[...] = st_ref[...].T


def _gla(q, k, v, la, rg, s0, s0_layer, g, layer, chunk):
    nb = s0.shape[1]
    t = q.shape[0] // nb
    carry = t > chunk
    if carry:
        tb = min(GLA_BLOCK, t)
        grid = (nb, t // tb)
        state = pl.BlockSpec((None, B_KEYS, B_VAL_DIM), lambda b, i: (b, 0, 0))
        state_in = pl.BlockSpec((None, None, B_KEYS, B_VAL_DIM), lambda b, i: (s0_layer, b, 0, 0))
    else:
        assert t == chunk
        streams = math.gcd(nb, max(1, GLA_BLOCK // t))
        tb = streams * t
        grid = (nb // streams, 1)
        state = pl.BlockSpec((streams, B_KEYS, B_VAL_DIM), lambda b, i: (b, 0, 0))
        state_in = pl.BlockSpec((None, streams, B_KEYS, B_VAL_DIM), lambda b, i: (s0_layer, b, 0, 0))
    nblk = grid[1]
    tok = lambda n: pl.BlockSpec((tb, n), lambda b, i: (b * nblk + i, 0))
    return pl.pallas_call(
        functools.partial(_gla_kernel, chunk=chunk, carry=carry),
        grid=grid,
        in_specs=[tok(B_KEYS), tok(B_KEYS), tok(B_WIDTH), tok(B_KEYS), tok(B_WIDTH), state_in, _pick(g, layer)],
        out_specs=[tok(B_WIDTH), state],
        out_shape=[jax.ShapeDtypeStruct((nb * t, B_WIDTH), BF16),
                   jax.ShapeDtypeStruct((nb, B_KEYS, B_VAL_DIM), F32)],
        scratch_shapes=[pltpu.VMEM((B_VAL_DIM, B_KEYS), F32)],
        compiler_params=_params("parallel", "arbitrary"),
        name="gla",
    )(q, k, v, la, rg, s0, g)


def _merge_kernel(x_ref, oa_ref, ob_ref, gpre_ref, gpost_ref, wgab_ref, woa_ref, wob_ref, wo_ref, o_ref):
    tm, d = x_ref.shape
    halves = [slice(i * (tm // MERGE_SPLIT), (i + 1) * (tm // MERGE_SPLIT)) for i in range(MERGE_SPLIT)]
    nt = (((1,), (1,)), ((), ()))
    first = []
    for r in halves:
        h = _rms(x_ref[r, :], gpre_ref[...], NORM_EPS).astype(BF16)
        first.append((lax.dot_general(h, wgab_ref[0], nt, preferred_element_type=F32),
                      jnp.dot(oa_ref[r, :], woa_ref[...], preferred_element_type=F32),
                      jnp.dot(ob_ref[r, :], wob_ref[...], preferred_element_type=F32)))
    mixed = []
    for gab, ya, yb in first:
        m = jax.nn.sigmoid(gab[:, :d]) * ya + jax.nn.sigmoid(gab[:, d:]) * yb
        mixed.append(jnp.dot(m.astype(BF16), wo_ref[...], preferred_element_type=F32))
    for r, y in zip(halves, mixed):
        o_ref[r, :] = x_ref[r, :] + _rms(y, gpost_ref[...], NORM_EPS)


def _merge(x, oa, ob, norm, w_t, woa, wob, wo, layer):
    t, d = x.shape
    tm = min(MERGE_ROW_TILE, t)
    row = lambda n: pl.BlockSpec((tm, n), lambda i: (i, 0))
    gate_rows = w_t.shape[1] - 2 * d
    w_gates = pl.BlockSpec((pl.Element(1), pl.Element(2 * d), pl.Element(d)), lambda i: (layer, gate_rows, 0),
                           pipeline_mode=pl.Buffered(1))
    return pl.pallas_call(
        _merge_kernel,
        grid=(t // tm,),
        in_specs=[row(d), row(A_WIDTH), row(B_WIDTH), _pick(norm, layer, 2), _pick(norm, layer, 3),
                  w_gates, _pick(woa, layer), _pick(wob, layer), _pick(wo, layer)],
        out_specs=row(d),
        out_shape=jax.ShapeDtypeStruct((t, d), F32),
        compiler_params=_params("parallel"),
        name="mixer_merge",
    )(x, oa, ob, norm, norm, w_t, woa, wob, wo)


def _rope_tables(pos):
    half = A_HEAD_DIM // 2
    inv = jnp.exp(-math.log(ROPE_THETA) * jnp.arange(half, dtype=F32) / half)
    ang = pos.astype(F32)[:, None] * inv[None, :]
    cos, sin = jnp.cos(ang), jnp.sin(ang)
    reps = LANES // A_HEAD_DIM
    return (jnp.tile(jnp.concatenate([cos, cos], axis=-1), (1, reps)),
            jnp.tile(jnp.concatenate([-sin, sin], axis=-1), (1, reps)))


def kernel(x_prompt, x_sample, cache_k, cache_v, state_gla, norm_g, ffn_w_gate, ffn_w_up, ffn_w_down, w_in,
           w_gate_up, b_gate, lambda_p, subln_g, gla_norm_g, w_out_a, w_out_b, w_out):
    depth = w_in.shape[0]
    bp, tp, d = x_prompt.shape
    bs, ts, _ = x_sample.shape
    past = cache_k.shape[2]
    assert bp == 1 and tp % CHUNK == 0

    n_qk = 2 * A_HEADS * A_HEAD_DIM
    assert w_in.shape[2] == 2 * n_qk + A_WIDTH + 2 * B_KEYS + 2 * B_WIDTH + GATE_RANK + 2 * d

    norm = norm_g.reshape(depth, norm_g.shape[1], 1, d)
    wg, wu, wd = ffn_w_gate.astype(BF16), ffn_w_up.astype(BF16), ffn_w_down.astype(BF16)
    w_t = jnp.transpose(w_in, (0, 2, 1)).astype(BF16)
    w_gu = w_gate_up.astype(BF16)
    bg = b_gate[:, None, :]
    woa, wob, wo = w_out_a.astype(BF16), w_out_b.astype(BF16), w_out.astype(BF16)
    sg = subln_g[:, None, :]
    gg = gla_norm_g[:, None, :]

    cos_p, sin_p = _rope_tables(jnp.arange(tp))
    cos_s, sin_s = _rope_tables(jnp.tile(past + jnp.arange(ts), bs))

    cache_kt = jnp.transpose(cache_k, (0, 1, 3, 4, 2))
    cache_vr = cache_v.reshape(depth, bs, past * A_HEADS, A_V_DIM)
    s0_sample = state_gla.reshape(depth, bs, B_KEYS, B_VAL_DIM)
    s0_prompt = jnp.zeros((1, bp, B_KEYS, B_VAL_DIM), F32)

    xp = x_prompt.reshape(tp, d)
    xs = x_sample.reshape(bs * ts, d)
    kv_p = kv_s = None
    states_p, states_s = [], []
    for l in range(depth):
        lam_init = 0.8 - 0.6 * math.exp(-0.3 * l)

        def layer(x, cos, sin, chunk, s0, s0_layer, kv, is_prompt):
            x = _ffn(x, norm, wg, wu, wd, l, 0)
            kf, vf, q1, q2, kb, vt, qg, kg, vg, rg, la = _proj(x, norm, w_t, w_gu, bg, cos, sin, l, is_prompt, kv)
            if is_prompt:
                oa = _attn_prompt(q1, q2, kb, vt, lambda_p, sg, l, lam_init)
            else:
                oa = _attn_sample(q1, q2, kb, vf, cache_kt, cache_vr, l, lambda_p, sg, lam_init)
            ob, s_new = _gla(qg, kg, vg, la, rg, s0, s0_layer, gg, l, chunk)
            x = _merge(x, oa, ob, norm, w_t, woa, wob, wo, l)
            x = _ffn(x, norm, wg, wu, wd, l, 1)
            return x, (kf, vf), s_new

        xp, kv_p, sp = layer(xp, cos_p, sin_p, CHUNK, s0_prompt, 0, kv_p, True)
        xs, kv_s, sn = layer(xs, cos_s, sin_s, ts, s0_sample, l, kv_s, False)
        states_p.append(sp)
        states_s.append(sn)

    k_prompt = jnp.transpose(kv_p[0].reshape(depth, bp, 2 * A_HEADS, A_HEAD_DIM, tp), (0, 1, 4, 2, 3))
    v_prompt = kv_p[1].reshape(depth, bp, tp, A_HEADS, A_V_DIM)
    s_prompt = jnp.stack(states_p).reshape(depth, bp, B_HEADS, B_KEY_DIM, B_VAL_DIM)
    k_sample = kv_s[0].reshape(depth, bs, ts, 2 * A_HEADS, A_HEAD_DIM)
    v_sample = kv_s[1].reshape(depth, bs, ts, A_HEADS, A_V_DIM)
    s_sample = jnp.stack(states_s).reshape(depth, bs, B_HEADS, B_KEY_DIM, B_VAL_DIM)
    return (xp.reshape(bp, tp, d), xs.reshape(bs, ts, d), k_prompt, v_prompt, s_prompt,
            k_sample, v_sample, s_sample)
```

```python
import functools
import math

import jax
import jax.numpy as jnp
import numpy as np
from jax import lax
from jax.experimental import pallas as pl
from jax.experimental.pallas import tpu as pltpu

F32 = jnp.float32
BF16 = jnp.bfloat16

CHUNK = 64
A_HEADS = 4
A_HEAD_DIM = 64
A_V_DIM = 2 * A_HEAD_DIM
A_WIDTH = A_HEADS * A_V_DIM
B_HEADS = 4
B_KEY_DIM = 64
B_VAL_DIM = 128
B_KEYS = B_HEADS * B_KEY_DIM
B_WIDTH = B_HEADS * B_VAL_DIM
GATE_RANK = 16
GATE_TAU = 16.0
ROPE_THETA = 10000.0
NORM_EPS = 1e-6
SUBLN_EPS = 1e-5
LANES = 128
MASK_VALUE = -1e30

ROW_TILE = 512
FFN_ROW_TILE = 1024
ROW_GROUP = 256
MERGE_ROW_TILE = 1024
MERGE_SPLIT = 2
ATTN_Q_TILE = 512
ATTN_K_TILE = 1024
ATTN_K_GROUP = 2
ATTN_SCORE_SLOTS = 3
V_ROWS = A_V_DIM + 16
LOG2E = math.log2(math.e)
GLA_BLOCK = 1024
GLA_SPAN = 256
VMEM_LIMIT = 56 * 1024 * 1024


def _rms(x, g, eps):
    return x * lax.rsqrt(jnp.mean(x * x, axis=-1, keepdims=True) + eps) * g


def _params(*sem):
    return pltpu.CompilerParams(dimension_semantics=sem, vmem_limit_bytes=VMEM_LIMIT)


def _row_groups(rows, group):
    return [slice(i, i + group) for i in range(0, rows, group)]


def _ffn_kernel(x_ref, gpre_ref, gpost_ref, wg_ref, wu_ref, wd_ref, o_ref):
    groups = _row_groups(x_ref.shape[0], min(x_ref.shape[0], ROW_GROUP))
    gate_up = []
    for r in groups:
        h = _rms(x_ref[r, :], gpre_ref[...], NORM_EPS).astype(BF16)
        gate_up.append((jnp.dot(h, wg_ref[...], preferred_element_type=F32),
                        jnp.dot(h, wu_ref[...], preferred_element_type=F32)))
    down = []
    for g, u in gate_up:
        a = (g * jax.nn.sigmoid(g) * u).astype(BF16)
        down.append(jnp.dot(a, wd_ref[...], preferred_element_type=F32))
    for r, y in zip(groups, down):
        o_ref[r, :] = x_ref[r, :] + 0.5 * _rms(y, gpost_ref[...], NORM_EPS)


def _pick(a, *lead):
    rest = a.shape[len(lead):]
    index = tuple(lead) + (0,) * len(rest)
    return pl.BlockSpec((None,) * len(lead) + rest, lambda *_: index, pipeline_mode=pl.Buffered(1))


def _ffn(x, norm, wg, wu, wd, layer, which):
    t, d = x.shape
    tm = min(FFN_ROW_TILE, t)
    row = pl.BlockSpec((tm, d), lambda i: (i, 0))
    return pl.pallas_call(
        _ffn_kernel,
        grid=(t // tm,),
        in_specs=[row, _pick(norm, layer, 4 * which), _pick(norm, layer, 4 * which + 1),
                  _pick(wg, layer, which), _pick(wu, layer, which), _pick(wd, layer, which)],
        out_specs=row,
        out_shape=jax.ShapeDtypeStruct((t, d), F32),
        compiler_params=_params("parallel"),
        name="ffn",
    )(x, norm, norm, wg, wu, wd)


def _rope(z, cos, sin_signed, first_half):
    rot = jnp.where(first_half, pltpu.roll(z, LANES - A_HEAD_DIM // 2, 1), pltpu.roll(z, A_HEAD_DIM // 2, 1))
    return z * cos + rot * sin_signed


def _proj_kernel(x_ref, g_ref, w_ref, wc_ref, wgu_ref, bg_ref, cos_ref, sin_ref, *rest, k_transposed, layer, n_alias):
    (kf_ref, vf_ref, q1_ref, q2_ref, kb_ref, vt_ref, qg_ref, kg_ref, vg_ref, rg_ref, la_ref) = rest[n_alias:]
    nt = (((1,), (1,)), ((), ()))
    if n_alias == 0:
        for j in range(kf_ref.shape[0]):
            if j != layer:
                kf_ref[j] = jnp.zeros(kf_ref.shape[1:], F32)
                vf_ref[j] = jnp.zeros(vf_ref.shape[1:], F32)
        kf_ref, vf_ref = kf_ref.at[layer], vf_ref.at[layer]
    h = _rms(x_ref[...], g_ref[...], NORM_EPS).astype(BF16)
    n_att = 4 * A_HEADS * A_HEAD_DIM + A_WIDTH
    code = lax.dot_general(h, wc_ref[...], nt, preferred_element_type=F32).astype(BF16)
    z = lax.dot_general(h, w_ref[0:n_att, :], nt, preferred_element_type=F32)
    gate = jnp.dot(code, wgu_ref[...], preferred_element_type=F32) + bg_ref[...]
    la_ref[...] = (jnp.minimum(gate, 0.0) - jnp.log1p(jnp.exp(-jnp.abs(gate)))) * (1.0 / GATE_TAU)
    zg = lax.dot_general(h, w_ref[n_att:, :], nt, preferred_element_type=F32)
    cos = cos_ref[...]
    sin = sin_ref[...]
    lane = lax.broadcasted_iota(jnp.int32, cos.shape, 1)
    first_half = (lane % A_HEAD_DIM) < (A_HEAD_DIM // 2)
    even_head = lane < A_HEAD_DIM
    scale = A_HEAD_DIM ** -0.5 * LOG2E
    o = 0
    for hh in range(A_HEADS):
        sl = slice(hh * LANES, (hh + 1) * LANES)
        q = _rope(z[:, o + hh * LANES:o + (hh + 1) * LANES], cos, sin, first_half) * scale
        q1_ref[:, sl] = jnp.where(even_head, q, 0.0).astype(BF16)
        q2_ref[:, sl] = jnp.where(even_head, 0.0, q).astype(BF16)
    o += 2 * A_HEADS * A_HEAD_DIM
    for hh in range(A_HEADS):
        sl = slice(hh * LANES, (hh + 1) * LANES)
        k = _rope(z[:, o + hh * LANES:o + (hh + 1) * LANES], cos, sin, first_half)
        if k_transposed:
            kf_ref[sl, :] = k.T
        else:
            kf_ref[:, sl] = k
        kb_ref[:, sl] = k.astype(BF16)
    o += 2 * A_HEADS * A_HEAD_DIM
    va = z[:, o:o + A_WIDTH]
    for hh in range(A_HEADS):
        vf_ref[pl.ds(hh, va.shape[0], stride=A_HEADS), :] = va[:, hh * A_V_DIM:(hh + 1) * A_V_DIM]
        vt_ref[hh * V_ROWS:hh * V_ROWS + A_V_DIM, :] = va[:, hh * A_V_DIM:(hh + 1) * A_V_DIM].T.astype(BF16)
        vt_ref[hh * V_ROWS + A_V_DIM:(hh + 1) * V_ROWS, :] = jnp.ones((V_ROWS - A_V_DIM, va.shape[0]), BF16)
    o = 0
    qg_ref[...] = zg[:, o:o + B_KEYS] * (B_KEY_DIM ** -0.5)
    o += B_KEYS
    kg_ref[...] = zg[:, o:o + B_KEYS]
    o += B_KEYS
    vg_ref[...] = zg[:, o:o + B_WIDTH].astype(BF16)
    o += B_WIDTH
    rg_ref[...] = zg[:, o:o + B_WIDTH]


def _proj(x, norm, w_t, wgu, bg, cos, sin, layer, k_transposed, kv_prev):
    t, d = x.shape
    depth = w_t.shape[0]
    tm = min(ROW_TILE, t)
    row = lambda n: pl.BlockSpec((tm, n), lambda i: (i, 0))
    col = lambda n: pl.BlockSpec((n, tm), lambda i: (0, i))
    sds = jax.ShapeDtypeStruct
    n_qk = 2 * A_HEADS * A_HEAD_DIM
    n_main = 2 * n_qk + A_WIDTH + 2 * B_KEYS + 2 * B_WIDTH
    assert n_main % GATE_RANK == 0
    k_shape = (n_qk, t) if k_transposed else (t, n_qk)
    k_block = (n_qk, tm) if k_transposed else (tm, n_qk)
    k_index = (lambda i: (0, i)) if k_transposed else (lambda i: (i, 0))
    v_shape, v_block = (t * A_HEADS, A_V_DIM), (tm * A_HEADS, A_V_DIM)
    if kv_prev is None:
        kv_specs = [pl.BlockSpec((depth,) + k_block, lambda i: (0,) + k_index(i)),
                    pl.BlockSpec((depth,) + v_block, lambda i: (0, i, 0))]
        alias_specs, alias_args, aliases = [], (), {}
    else:
        kv_specs = [pl.BlockSpec((None,) + k_block, lambda i: (layer,) + k_index(i)),
                    pl.BlockSpec((None,) + v_block, lambda i: (layer, i, 0))]
        alias_specs = [pl.BlockSpec(memory_space=pl.ANY)] * 2
        alias_args, aliases = tuple(kv_prev), {8: 0, 9: 1}
    return pl.pallas_call(
        functools.partial(_proj_kernel, k_transposed=k_transposed, layer=layer, n_alias=len(alias_args)),
        grid=(t // tm,),
        in_specs=[row(d), _pick(norm, layer, 2),
                  pl.BlockSpec((None, n_main, d), lambda i: (layer, 0, 0), pipeline_mode=pl.Buffered(1)),
                  pl.BlockSpec((None, GATE_RANK, d), lambda i: (layer, n_main // GATE_RANK, 0),
                               pipeline_mode=pl.Buffered(1)),
                  _pick(wgu, layer), _pick(bg, layer), row(LANES), row(LANES)] + alias_specs,
        out_specs=kv_specs + [row(A_WIDTH), row(A_WIDTH), row(A_WIDTH), col(A_HEADS * V_ROWS),
                              row(B_KEYS), row(B_KEYS), row(B_WIDTH), row(B_WIDTH), row(B_KEYS)],
        out_shape=[sds((depth,) + k_shape, F32), sds((depth,) + v_shape, F32),
                   sds((t, A_WIDTH), BF16), sds((t, A_WIDTH), BF16), sds((t, A_WIDTH), BF16),
                   sds((A_HEADS * V_ROWS, t), BF16),
                   sds((t, B_KEYS), F32), sds((t, B_KEYS), F32), sds((t, B_WIDTH), BF16),
                   sds((t, B_WIDTH), F32), sds((t, B_KEYS), F32)],
        input_output_aliases=aliases,
        compiler_params=_params("parallel"),
        name="mixer_proj",
    )(x, norm, w_t, w_t, wgu, bg, cos, sin, *alias_args)


def _diff_lambda(lp, lam_init):
    a = jnp.sum(lp[0:1] * lp[1:2], axis=-1, keepdims=True)
    b = jnp.sum(lp[2:3] * lp[3:4], axis=-1, keepdims=True)
    return jnp.exp(a) - jnp.exp(b) + lam_init


def _subln(o, g, lam_init):
    return _rms(o, g, SUBLN_EPS) * (1.0 - lam_init)


def _attn_prompt_kernel(qi_ref, kg_ref, fl_ref, q1_ref, q2_ref, k_ref, vt_ref, lp_ref, g_ref,
                        o_ref, m_ref, acc_ref, st_ref, *, lam_init, variants):
    n = pl.program_id(0)
    slots, tk, tq = st_ref.shape
    group = k_ref.shape[0] // tk
    flags = fl_ref[n]
    n_maps = 2 * A_HEADS
    nt = (((1,), (1,)), ((), ()))

    @pl.when((flags & 1) != 0)
    def _():
        m_ref[...] = jnp.full(m_ref.shape, MASK_VALUE, F32)
        acc_ref[...] = jnp.zeros(acc_ref.shape, F32)

    def step(n_present, n_masked):
        penalty = {}
        if n_masked:
            n_chunks = tk // CHUNK
            assert n_chunks <= A_HEAD_DIM
            lane_k = lax.broadcasted_iota(jnp.int32, (tk, LANES), 1)
            chunk_k = lax.broadcasted_iota(jnp.int32, (tk, LANES), 0) // CHUNK
            lane_q = lax.broadcasted_iota(jnp.int32, (tq, LANES), 1)
            chunk_q = (qi_ref[n] * tq) // CHUNK + lax.broadcasted_iota(jnp.int32, (tq, LANES), 0) // CHUNK
            keep, onehot = [], []
            for c in range(2):
                off = A_HEAD_DIM * (1 - c)
                own = (lane_k < A_HEAD_DIM) if c == 0 else (lane_k >= A_HEAD_DIM)
                keep.append(jnp.where(own, 1.0, 0.0).astype(BF16))
                onehot.append(jnp.where(lane_k - off == chunk_k, 1.0, 0.0).astype(BF16))
                j = lane_q - off
                for s in range(n_present - n_masked, n_present):
                    first_chunk = ((kg_ref[n] * group + s) * tk) // CHUNK
                    hidden = jnp.where(j >= 0, jnp.where(j < n_chunks, first_chunk + j, -1), -1) > chunk_q
                    penalty[s, c] = jnp.where(hidden, MASK_VALUE, 0.0).astype(BF16)

        units = [(s, idx) for s in range(n_present) for idx in range(n_maps)]

        def scores(u):
            s, idx = units[u]
            hh, c = divmod(idx, 2)
            sl = slice(hh * LANES, (hh + 1) * LANES)
            k = k_ref[s * tk:(s + 1) * tk, sl]
            q = (q1_ref, q2_ref)[c][:, sl]
            if (s, c) in penalty:
                k = k * keep[c] + onehot[c]
                q = q + penalty[s, c]
            st_ref[u % slots] = lax.dot_general(k, q, nt, preferred_element_type=F32)

        def accumulate(u):
            s, idx = units[u]
            hh = idx // 2
            st = st_ref[u % slots]
            m_old = m_ref[idx]
            m_new = jnp.maximum(m_old, jnp.max(st, axis=0, keepdims=True))
            m_ref[idx] = m_new
            p = jnp.exp2(st - m_new).astype(BF16)
            pv = jnp.dot(vt_ref[hh * V_ROWS:(hh + 1) * V_ROWS, s * tk:(s + 1) * tk], p, preferred_element_type=F32)
            acc_ref[idx] = jnp.exp2(m_old - m_new) * acc_ref[idx] + pv

        ahead = slots - 1
        for u in range(min(ahead, len(units))):
            scores(u)
        for u in range(len(units)):
            if u + ahead < len(units):
                scores(u + ahead)
            accumulate(u)

    for v, (n_present, n_masked) in enumerate(variants):
        @pl.when((flags >> 2) == v)
        def _():
            step(n_present, n_masked)

    @pl.when((flags & 2) != 0)
    def _():
        lam = _diff_lambda(lp_ref[...], lam_init)
        for hh in range(A_HEADS):
            outs = []
            for idx in (2 * hh, 2 * hh + 1):
                outs.append(acc_ref[idx, 0:A_V_DIM, :] * (1.0 / acc_ref[idx, A_V_DIM:A_V_DIM + 1, :]))
            o = (outs[0] - lam * outs[1]).T
            o_ref[:, hh * LANES:(hh + 1) * LANES] = _subln(o, g_ref[...], lam_init).astype(BF16)


def _attn_steps(t, tq, tk, group):
    qi, kg, fl, variants = [], [], [], []
    for i in range(t // tq):
        q_lo, q_hi = (i * tq) // CHUNK, (i * tq + tq - 1) // CHUNK
        js = [j for j in range(t // tk) if (j * tk) // CHUNK <= q_hi]
        for g in range(0, len(js), group):
            members = js[g:g + group]
            variant = (len(members), sum((j * tk + tk - 1) // CHUNK > q_lo for j in members))
            if variant not in variants:
                variants.append(variant)
            qi.append(i)
            kg.append(g // group)
            fl.append((1 if g == 0 else 0) | (2 if g + group >= len(js) else 0) | (variants.index(variant) << 2))
    return (np.asarray(qi, np.int32), np.asarray(kg, np.int32), np.asarray(fl, np.int32)), tuple(variants)


def _attn_prompt(q1, q2, kb, vt, lam_p, subln_g, layer, lam_init):
    t = q1.shape[0]
    tq, tk = min(ATTN_Q_TILE, t), min(ATTN_K_TILE, t)
    group = ATTN_K_GROUP if t % (ATTN_K_GROUP * tk) == 0 else 1
    (qi, kg, fl), variants = _attn_steps(t, tq, tk, group)
    grid_spec = pltpu.PrefetchScalarGridSpec(
        num_scalar_prefetch=3,
        grid=(len(qi),),
        in_specs=[
            pl.BlockSpec((tq, A_WIDTH), lambda n, qi, kg, fl: (qi[n], 0)),
            pl.BlockSpec((tq, A_WIDTH), lambda n, qi, kg, fl: (qi[n], 0)),
            pl.BlockSpec((group * tk, A_WIDTH), lambda n, qi, kg, fl: (kg[n], 0)),
            pl.BlockSpec((A_HEADS * V_ROWS, group * tk), lambda n, qi, kg, fl: (0, kg[n])),
            _pick(lam_p, layer),
            _pick(subln_g, layer),
        ],
        out_specs=pl.BlockSpec((tq, A_WIDTH), lambda n, qi, kg, fl: (qi[n], 0)),
        scratch_shapes=[pltpu.VMEM((2 * A_HEADS, 1, tq), F32),
                        pltpu.VMEM((2 * A_HEADS, V_ROWS, tq), F32),
                        pltpu.VMEM((ATTN_SCORE_SLOTS, tk, tq), F32)],
    )
    return pl.pallas_call(
        functools.partial(_attn_prompt_kernel, lam_init=lam_init, variants=variants),
        grid_spec=grid_spec,
        out_shape=jax.ShapeDtypeStruct((t, A_WIDTH), BF16),
        compiler_params=_params("arbitrary"),
        name="attn_prompt",
    )(jnp.asarray(qi), jnp.asarray(kg), jnp.asarray(fl), q1, q2, kb, vt, lam_p, subln_g)


def _attn_sample_kernel(q1_ref, q2_ref, kn_ref, vn_ref, ckt_ref, cv_ref, lp_ref, g_ref, o_ref, *, lam_init):
    past = cv_ref.shape[0] // A_HEADS
    tq = q1_ref.shape[0]
    qpos = past + lax.broadcasted_iota(jnp.int32, (tq, past), 0)
    vis_past = (lax.broadcasted_iota(jnp.int32, (tq, past), 1) // CHUNK) <= (qpos // CHUNK)
    qpos_n = lax.broadcasted_iota(jnp.int32, (tq, tq), 0)
    vis_new = (lax.broadcasted_iota(jnp.int32, (tq, tq), 1) + past) // CHUNK <= (qpos_n + past) // CHUNK
    lam = _diff_lambda(lp_ref[...], lam_init)
    nt = (((1,), (1,)), ((), ()))
    scores = []
    for hh in range(A_HEADS):
        sl = slice(hh * LANES, (hh + 1) * LANES)
        ckt = ckt_ref[2 * hh:2 * hh + 2].reshape(2 * A_HEAD_DIM, past).astype(BF16)
        kn = kn_ref[:, sl]
        for q_ref in (q1_ref, q2_ref):
            q = q_ref[:, sl]
            scores.append((jnp.where(vis_past, jnp.dot(q, ckt, preferred_element_type=F32), MASK_VALUE),
                           jnp.where(vis_new, lax.dot_general(q, kn, nt, preferred_element_type=F32), MASK_VALUE)))
    probs = []
    for s_p, s_n in scores:
        m = jnp.maximum(jnp.max(s_p, axis=-1, keepdims=True), jnp.max(s_n, axis=-1, keepdims=True))
        p_p = jnp.exp2(s_p - m)
        p_n = jnp.exp2(s_n - m)
        l = jnp.sum(p_p, axis=-1, keepdims=True) + jnp.sum(p_n, axis=-1, keepdims=True)
        probs.append((p_p.astype(BF16), p_n.astype(BF16), 1.0 / l))
    for hh in range(A_HEADS):
        cv = cv_ref[pl.ds(hh, past, stride=A_HEADS), :].astype(BF16)
        vn = vn_ref[pl.ds(hh, tq, stride=A_HEADS), :].astype(BF16)
        outs = []
        for p_p, p_n, inv_l in probs[2 * hh:2 * hh + 2]:
            pv = jnp.dot(p_p, cv, preferred_element_type=F32) + jnp.dot(p_n, vn, preferred_element_type=F32)
            outs.append(pv * inv_l)
        o = outs[0] - lam * outs[1]
        o_ref[:, hh * LANES:(hh + 1) * LANES] = _subln(o, g_ref[...], lam_init).astype(BF16)


def _attn_sample(q1, q2, kb, vf, cache_kt, cache_v, layer, lam_p, subln_g, lam_init):
    _, nb, nh2, hd, past = cache_kt.shape
    tq = q1.shape[0] // nb
    new = lambda: pl.BlockSpec((tq, A_WIDTH), lambda b: (b, 0))
    return pl.pallas_call(
        functools.partial(_attn_sample_kernel, lam_init=lam_init),
        grid=(nb,),
        in_specs=[new(), new(), new(), pl.BlockSpec((None, tq * A_HEADS, A_V_DIM), lambda b: (layer, b, 0)),
                  pl.BlockSpec((None, None, nh2, hd, past), lambda b: (layer, b, 0, 0, 0)),
                  pl.BlockSpec((None, None, past * A_HEADS, A_V_DIM), lambda b: (layer, b, 0, 0)),
                  _pick(lam_p, layer), _pick(subln_g, layer)],
        out_specs=new(),
        out_shape=jax.ShapeDtypeStruct((nb * tq, A_WIDTH), BF16),
        compiler_params=_params("parallel"),
        name="attn_sample",
    )(q1, q2, kb, vf, cache_kt, cache_v, lam_p, subln_g)


def _gla_kernel(q_ref, k_ref, v_ref, la_ref, r_ref, s0_ref, g_ref, o_ref, s_ref, st_ref, *, chunk, carry):
    blk = pl.program_id(1)
    tokens = q_ref.shape[0]

    if carry:
        @pl.when(blk == 0)
        def _():
            st_ref[...] = s0_ref[...].T

    span = min(tokens, max(chunk, GLA_SPAN))
    row = lax.broadcasted_iota(jnp.int32, (span, span), 0)
    col = lax.broadcasted_iota(jnp.int32, (span, span), 1)
    same_chunk = (row // chunk) == (col // chunk)
    causal = jnp.where(same_chunk, col, span) <= row
    tri = jnp.where(causal, 1.0, 0.0).astype(BF16)
    ones = jnp.where(same_chunk, 1.0, 0.0).astype(BF16)
    lane = lax.broadcasted_iota(jnp.int32, (span, B_KEYS), 1)
    nt = (((1,), (1,)), ((), ()))
    tn = (((0,), (0,)), ((), ()))

    groups = []
    for g0 in range(0, tokens, span):
        rs = slice(g0, g0 + span)
        la = la_ref[rs, :]
        la_hi = la.astype(BF16)
        la_lo = (la - la_hi.astype(F32)).astype(BF16)
        b = (jnp.dot(tri, la_hi, preferred_element_type=F32)
             + jnp.dot(tri, la_lo, preferred_element_type=F32))
        total = (jnp.dot(ones, la_hi, preferred_element_type=F32)
                 + jnp.dot(ones, la_lo, preferred_element_type=F32))
        qe = q_ref[rs, :] * jnp.exp(b)
        k = k_ref[rs, :]
        ke = (k * jnp.exp(-b)).astype(BF16)
        kd = k * jnp.exp(total - b)
        qe_h, kd_h, o_intra = [], [], []
        for hh in range(B_HEADS):
            in_head = (lane // B_KEY_DIM) == hh
            qe_h.append(jnp.where(in_head, qe, 0.0).astype(BF16))
            kd_h.append(jnp.where(in_head, kd, 0.0).astype(BF16))
            att = lax.dot_general(qe_h[hh], ke, nt, preferred_element_type=F32)
            att = jnp.where(causal, att, 0.0).astype(BF16)
            o_intra.append(jnp.dot(att, v_ref[rs, hh * B_VAL_DIM:(hh + 1) * B_VAL_DIM], preferred_element_type=F32))
        groups.append((qe_h, kd_h, o_intra, total))

    n_chunks = tokens // chunk
    where = [(groups[c * chunk // span], slice(c * chunk % span, c * chunk % span + chunk)) for c in range(n_chunks)]
    increments = []
    for c, ((_, kd_h, _, _), rl) in enumerate(where):
        r = slice(c * chunk, (c + 1) * chunk)
        kd_stack = jnp.concatenate([kd_h[hh][rl] for hh in range(B_HEADS)], axis=0)
        v_stack = jnp.concatenate([v_ref[r, hh * B_VAL_DIM:(hh + 1) * B_VAL_DIM] for hh in range(B_HEADS)], axis=0)
        increments.append(lax.dot_general(v_stack, kd_stack, tn, preferred_element_type=F32))
    states = []
    st = st_ref[...] if carry else None
    for c, ((_, _, _, total), rl) in enumerate(where):
        if not carry:
            st = s0_ref[c].T
        states.append(st)
        st = st * jnp.exp(total[rl.start:rl.start + 1, :]) + increments[c]
        if not carry:
            s_ref[c] = st.T
    if carry:
        st_ref[...] = st
    for c, ((qe_h, _, o_intra, _), rl) in enumerate(where):
        r = slice(c * chunk, (c + 1) * chunk)
        q_stack = jnp.concatenate([qe_h[hh][rl] for hh in range(B_HEADS)], axis=0)
        o_inter = lax.dot_general(q_stack, states[c].astype(BF16), nt, preferred_element_type=F32)
        for hh in range(B_HEADS):
            vs = slice(hh * B_VAL_DIM, (hh + 1) * B_VAL_DIM)
            o = o_intra[hh][rl] + o_inter[hh * chunk:(hh + 1) * chunk]
            rg = r_ref[r, vs]
            o_ref[r, vs] = (_rms(o, g_ref[...], SUBLN_EPS) * (rg * jax.nn.sigmoid(rg))).astype(BF16)

    if carry:
        @pl.when(blk == pl.num_programs(1) - 1)
        def _():
            s_ref[...] = st_ref[...].T


def _gla(q, k, v, la, rg, s0, s0_layer, g, layer, chunk):
    nb = s0.shape[1]
    t = q.shape[0] // nb
    carry = t > chunk
    if carry:
        tb = min(GLA_BLOCK, t)
        grid = (nb, t // tb)
        state = pl.BlockSpec((None, B_KEYS, B_VAL_DIM), lambda b, i: (b, 0, 0))
        state_in = pl.BlockSpec((None, None, B_KEYS, B_VAL_DIM), lambda b, i: (s0_layer, b, 0, 0))
    else:
        assert t == chunk
        streams = math.gcd(nb, max(1, GLA_BLOCK // t))
        tb = streams * t
        grid = (nb // streams, 1)
        state = pl.BlockSpec((streams, B_KEYS, B_VAL_DIM), lambda b, i: (b, 0, 0))
        state_in = pl.BlockSpec((None, streams, B_KEYS, B_VAL_DIM), lambda b, i: (s0_layer, b, 0, 0))
    nblk = grid[1]
    tok = lambda n: pl.BlockSpec((tb, n), lambda b, i: (b * nblk + i, 0))
    return pl.pallas_call(
        functools.partial(_gla_kernel, chunk=chunk, carry=carry),
        grid=grid,
        in_specs=[tok(B_KEYS), tok(B_KEYS), tok(B_WIDTH), tok(B_KEYS), tok(B_WIDTH), state_in, _pick(g, layer)],
        out_specs=[tok(B_WIDTH), state],
        out_shape=[jax.ShapeDtypeStruct((nb * t, B_WIDTH), BF16),
                   jax.ShapeDtypeStruct((nb, B_KEYS, B_VAL_DIM), F32)],
        scratch_shapes=[pltpu.VMEM((B_VAL_DIM, B_KEYS), F32)],
        compiler_params=_params("parallel", "arbitrary"),
        name="gla",
    )(q, k, v, la, rg, s0, g)


def _merge_kernel(x_ref, oa_ref, ob_ref, gpre_ref, gpost_ref, wgab_ref, woa_ref, wob_ref, wo_ref, o_ref):
    tm, d = x_ref.shape
    halves = [slice(i * (tm // MERGE_SPLIT), (i + 1) * (tm // MERGE_SPLIT)) for i in range(MERGE_SPLIT)]
    nt = (((1,), (1,)), ((), ()))
    first = []
    for r in halves:
        h = _rms(x_ref[r, :], gpre_ref[...], NORM_EPS).astype(BF16)
        first.append((lax.dot_general(h, wgab_ref[0], nt, preferred_element_type=F32),
                      jnp.dot(oa_ref[r, :], woa_ref[...], preferred_element_type=F32),
                      jnp.dot(ob_ref[r, :], wob_ref[...], preferred_element_type=F32)))
    mixed = []
    for gab, ya, yb in first:
        m = jax.nn.sigmoid(gab[:, :d]) * ya + jax.nn.sigmoid(gab[:, d:]) * yb
        mixed.append(jnp.dot(m.astype(BF16), wo_ref[...], preferred_element_type=F32))
    for r, y in zip(halves, mixed):
        o_ref[r, :] = x_ref[r, :] + _rms(y, gpost_ref[...], NORM_EPS)


def _merge(x, oa, ob, norm, w_t, woa, wob, wo, layer):
    t, d = x.shape
    tm = min(MERGE_ROW_TILE, t)
    row = lambda n: pl.BlockSpec((tm, n), lambda i: (i, 0))
    gate_rows = w_t.shape[1] - 2 * d
    w_gates = pl.BlockSpec((pl.Element(1), pl.Element(2 * d), pl.Element(d)), lambda i: (layer, gate_rows, 0),
                           pipeline_mode=pl.Buffered(1))
    return pl.pallas_call(
        _merge_kernel,
        grid=(t // tm,),
        in_specs=[row(d), row(A_WIDTH), row(B_WIDTH), _pick(norm, layer, 2), _pick(norm, layer, 3),
                  w_gates, _pick(woa, layer), _pick(wob, layer), _pick(wo, layer)],
        out_specs=row(d),
        out_shape=jax.ShapeDtypeStruct((t, d), F32),
        compiler_params=_params("parallel"),
        name="mixer_merge",
    )(x, oa, ob, norm, norm, w_t, woa, wob, wo)


def _rope_tables(pos):
    half = A_HEAD_DIM // 2
    inv = jnp.exp(-math.log(ROPE_THETA) * jnp.arange(half, dtype=F32) / half)
    ang = pos.astype(F32)[:, None] * inv[None, :]
    cos, sin = jnp.cos(ang), jnp.sin(ang)
    reps = LANES // A_HEAD_DIM
    return (jnp.tile(jnp.concatenate([cos, cos], axis=-1), (1, reps)),
            jnp.tile(jnp.concatenate([-sin, sin], axis=-1), (1, reps)))


def kernel(x_prompt, x_sample, cache_k, cache_v, state_gla, norm_g, ffn_w_gate, ffn_w_up, ffn_w_down, w_in,
           w_gate_up, b_gate, lambda_p, subln_g, gla_norm_g, w_out_a, w_out_b, w_out):
    depth = w_in.shape[0]
    bp, tp, d = x_prompt.shape
    bs, ts, _ = x_sample.shape
    past = cache_k.shape[2]
    assert bp == 1 and tp % CHUNK == 0

    n_qk = 2 * A_HEADS * A_HEAD_DIM
    assert w_in.shape[2] == 2 * n_qk + A_WIDTH + 2 * B_KEYS + 2 * B_WIDTH + GATE_RANK + 2 * d

    norm = norm_g.reshape(depth, norm_g.shape[1], 1, d)
    wg, wu, wd = ffn_w_gate.astype(BF16), ffn_w_up.astype(BF16), ffn_w_down.astype(BF16)
    w_t = jnp.transpose(w_in, (0, 2, 1)).astype(BF16)
    w_gu = w_gate_up.astype(BF16)
    bg = b_gate[:, None, :]
    woa, wob, wo = w_out_a.astype(BF16), w_out_b.astype(BF16), w_out.astype(BF16)
    sg = subln_g[:, None, :]
    gg = gla_norm_g[:, None, :]

    cos_p, sin_p = _rope_tables(jnp.arange(tp))
    cos_s, sin_s = _rope_tables(jnp.tile(past + jnp.arange(ts), bs))

    cache_kt = jnp.transpose(cache_k, (0, 1, 3, 4, 2))
    cache_vr = cache_v.reshape(depth, bs, past * A_HEADS, A_V_DIM)
    s0_sample = state_gla.reshape(depth, bs, B_KEYS, B_VAL_DIM)
    s0_prompt = jnp.zeros((1, bp, B_KEYS, B_VAL_DIM), F32)

    xp = x_prompt.reshape(tp, d)
    xs = x_sample.reshape(bs * ts, d)
    kv_p = kv_s = None
    states_p, states_s = [], []
    for l in range(depth):
        lam_init = 0.8 - 0.6 * math.exp(-0.3 * l)

        def layer(x, cos, sin, chunk, s0, s0_layer, kv, is_prompt):
            x = _ffn(x, norm, wg, wu, wd, l, 0)
            kf, vf, q1, q2, kb, vt, qg, kg, vg, rg, la = _proj(x, norm, w_t, w_gu, bg, cos, sin, l, is_prompt, kv)
            if is_prompt:
                oa = _attn_prompt(q1, q2, kb, vt, lambda_p, sg, l, lam_init)
            else:
                oa = _attn_sample(q1, q2, kb, vf, cache_kt, cache_vr, l, lambda_p, sg, lam_init)
            ob, s_new = _gla(qg, kg, vg, la, rg, s0, s0_layer, gg, l, chunk)
            x = _merge(x, oa, ob, norm, w_t, woa, wob, wo, l)
            x = _ffn(x, norm, wg, wu, wd, l, 1)
            return x, (kf, vf), s_new

        xp, kv_p, sp = layer(xp, cos_p, sin_p, CHUNK, s0_prompt, 0, kv_p, True)
        xs, kv_s, sn = layer(xs, cos_s, sin_s, ts, s0_sample, l, kv_s, False)
        states_p.append(sp)
        states_s.append(sn)

    k_prompt = jnp.transpose(kv_p[0].reshape(depth, bp, 2 * A_HEADS, A_HEAD_DIM, tp), (0, 1, 4, 2, 3))
    v_prompt = kv_p[1].reshape(depth, bp, tp, A_HEADS, A_V_DIM)
    s_prompt = jnp.stack(states_p).reshape(depth, bp, B_HEADS, B_KEY_DIM, B_VAL_DIM)
    k_sample = kv_s[0].reshape(depth, bs, ts, 2 * A_HEADS, A_HEAD_DIM)
    v_sample = kv_s[1].reshape(depth, bs, ts, A_HEADS, A_V_DIM)
    s_sample = jnp.stack(states_s).reshape(depth, bs, B_HEADS, B_KEY_DIM, B_VAL_DIM)
    return (xp.reshape(bp, tp, d), xs.reshape(bs, ts, d), k_prompt, v_prompt, s_prompt,
            k_sample, v_sample, s_sample)
```

```python
import functools
import math

import jax
import jax.numpy as jnp
import numpy as np
from jax import lax
from jax.experimental import pallas as pl
from jax.experimental.pallas import tpu as pltpu

F32 = jnp.float32
BF16 = jnp.bfloat16

CHUNK = 64
A_HEADS = 4
A_HEAD_DIM = 64
A_V_DIM = 2 * A_HEAD_DIM
A_WIDTH = A_HEADS * A_V_DIM
B_HEADS = 4
B_KEY_DIM = 64
B_VAL_DIM = 128
B_KEYS = B_HEADS * B_KEY_DIM
B_WIDTH = B_HEADS * B_VAL_DIM
GATE_RANK = 16
GATE_TAU = 16.0
ROPE_THETA = 10000.0
NORM_EPS = 1e-6
SUBLN_EPS = 1e-5
LANES = 128
MASK_VALUE = -1e30

ROW_TILE = 512
FFN_ROW_TILE = 1024
ROW_GROUP = 256
MERGE_ROW_TILE = 1024
MERGE_SPLIT = 2
ATTN_Q_TILE = 512
ATTN_K_TILE = 512
ATTN_K_GROUP = 4
ATTN_SCORE_SLOTS = 3
V_ROWS = A_V_DIM + 16
LOG2E = math.log2(math.e)
GLA_BLOCK = 1024
GLA_SPAN = 256
VMEM_LIMIT = 56 * 1024 * 1024


def _rms(x, g, eps):
    return x * lax.rsqrt(jnp.mean(x * x, axis=-1, keepdims=True) + eps) * g


def _params(*sem):
    return pltpu.CompilerParams(dimension_semantics=sem, vmem_limit_bytes=VMEM_LIMIT)


def _row_groups(rows, group):
    return [slice(i, i + group) for i in range(0, rows, group)]


def _ffn_kernel(x_ref, gpre_ref, gpost_ref, wg_ref, wu_ref, wd_ref, o_ref):
    groups = _row_groups(x_ref.shape[0], min(x_ref.shape[0], ROW_GROUP))
    gate_up = []
    for r in groups:
        h = _rms(x_ref[r, :], gpre_ref[...], NORM_EPS).astype(BF16)
        gate_up.append((jnp.dot(h, wg_ref[...], preferred_element_type=F32),
                        jnp.dot(h, wu_ref[...], preferred_element_type=F32)))
    down = []
    for g, u in gate_up:
        a = (g * jax.nn.sigmoid(g) * u).astype(BF16)
        down.append(jnp.dot(a, wd_ref[...], preferred_element_type=F32))
    for r, y in zip(groups, down):
        o_ref[r, :] = x_ref[r, :] + 0.5 * _rms(y, gpost_ref[...], NORM_EPS)


def _pick(a, *lead):
    rest = a.shape[len(lead):]
    index = tuple(lead) + (0,) * len(rest)
    return pl.BlockSpec((None,) * len(lead) + rest, lambda *_: index, pipeline_mode=pl.Buffered(1))


def _ffn(x, norm, wg, wu, wd, layer, which):
    t, d = x.shape
    tm = min(FFN_ROW_TILE, t)
    row = pl.BlockSpec((tm, d), lambda i: (i, 0))
    return pl.pallas_call(
        _ffn_kernel,
        grid=(t // tm,),
        in_specs=[row, _pick(norm, layer, 4 * which), _pick(norm, layer, 4 * which + 1),
                  _pick(wg, layer, which), _pick(wu, layer, which), _pick(wd, layer, which)],
        out_specs=row,
        out_shape=jax.ShapeDtypeStruct((t, d), F32),
        compiler_params=_params("parallel"),
        name="ffn",
    )(x, norm, norm, wg, wu, wd)


def _rope(z, cos, sin_signed, first_half):
    rot = jnp.where(first_half, pltpu.roll(z, LANES - A_HEAD_DIM // 2, 1), pltpu.roll(z, A_HEAD_DIM // 2, 1))
    return z * cos + rot * sin_signed


def _proj_kernel(x_ref, g_ref, w_ref, wc_ref, wgu_ref, bg_ref, cos_ref, sin_ref, *rest, k_transposed, layer, n_alias):
    (kf_ref, vf_ref, q1_ref, q2_ref, kb_ref, vt_ref, qg_ref, kg_ref, vg_ref, rg_ref, la_ref) = rest[n_alias:]
    nt = (((1,), (1,)), ((), ()))
    if n_alias == 0:
        for j in range(kf_ref.shape[0]):
            if j != layer:
                kf_ref[j] = jnp.zeros(kf_ref.shape[1:], F32)
                vf_ref[j] = jnp.zeros(vf_ref.shape[1:], F32)
        kf_ref, vf_ref = kf_ref.at[layer], vf_ref.at[layer]
    h = _rms(x_ref[...], g_ref[...], NORM_EPS).astype(BF16)
    n_att = 4 * A_HEADS * A_HEAD_DIM + A_WIDTH
    code = lax.dot_general(h, wc_ref[...], nt, preferred_element_type=F32).astype(BF16)
    z = lax.dot_general(h, w_ref[0:n_att, :], nt, preferred_element_type=F32)
    gate = jnp.dot(code, wgu_ref[...], preferred_element_type=F32) + bg_ref[...]
    la_ref[...] = (jnp.minimum(gate, 0.0) - jnp.log1p(jnp.exp(-jnp.abs(gate)))) * (1.0 / GATE_TAU)
    zg = lax.dot_general(h, w_ref[n_att:, :], nt, preferred_element_type=F32)
    cos = cos_ref[...]
    sin = sin_ref[...]
    lane = lax.broadcasted_iota(jnp.int32, cos.shape, 1)
    first_half = (lane % A_HEAD_DIM) < (A_HEAD_DIM // 2)
    even_head = lane < A_HEAD_DIM
    scale = A_HEAD_DIM ** -0.5 * LOG2E
    o = 0
    for hh in range(A_HEADS):
        sl = slice(hh * LANES, (hh + 1) * LANES)
        q = _rope(z[:, o + hh * LANES:o + (hh + 1) * LANES], cos, sin, first_half) * scale
        q1_ref[:, sl] = jnp.where(even_head, q, 0.0).astype(BF16)
        q2_ref[:, sl] = jnp.where(even_head, 0.0, q).astype(BF16)
    o += 2 * A_HEADS * A_HEAD_DIM
    for hh in range(A_HEADS):
        sl = slice(hh * LANES, (hh + 1) * LANES)
        k = _rope(z[:, o + hh * LANES:o + (hh + 1) * LANES], cos, sin, first_half)
        if k_transposed:
            kf_ref[sl, :] = k.T
        else:
            kf_ref[:, sl] = k
        kb_ref[:, sl] = k.astype(BF16)
    o += 2 * A_HEADS * A_HEAD_DIM
    va = z[:, o:o + A_WIDTH]
    for hh in range(A_HEADS):
        vf_ref[pl.ds(hh, va.shape[0], stride=A_HEADS), :] = va[:, hh * A_V_DIM:(hh + 1) * A_V_DIM]
        vt_ref[hh * V_ROWS:hh * V_ROWS + A_V_DIM, :] = va[:, hh * A_V_DIM:(hh + 1) * A_V_DIM].T.astype(BF16)
        vt_ref[hh * V_ROWS + A_V_DIM:(hh + 1) * V_ROWS, :] = jnp.ones((V_ROWS - A_V_DIM, va.shape[0]), BF16)
    o = 0
    qg_ref[...] = zg[:, o:o + B_KEYS] * (B_KEY_DIM ** -0.5)
    o += B_KEYS
    kg_ref[...] = zg[:, o:o + B_KEYS]
    o += B_KEYS
    vg_ref[...] = zg[:, o:o + B_WIDTH].astype(BF16)
    o += B_WIDTH
    rg_ref[...] = zg[:, o:o + B_WIDTH].astype(BF16)


def _proj(x, norm, w_t, wgu, bg, cos, sin, layer, k_transposed, kv_prev):
    t, d = x.shape
    depth = w_t.shape[0]
    tm = min(ROW_TILE, t)
    row = lambda n: pl.BlockSpec((tm, n), lambda i: (i, 0))
    col = lambda n: pl.BlockSpec((n, tm), lambda i: (0, i))
    sds = jax.ShapeDtypeStruct
    n_qk = 2 * A_HEADS * A_HEAD_DIM
    n_main = 2 * n_qk + A_WIDTH + 2 * B_KEYS + 2 * B_WIDTH
    assert n_main % GATE_RANK == 0
    k_shape = (n_qk, t) if k_transposed else (t, n_qk)
    k_block = (n_qk, tm) if k_transposed else (tm, n_qk)
    k_index = (lambda i: (0, i)) if k_transposed else (lambda i: (i, 0))
    v_shape, v_block = (t * A_HEADS, A_V_DIM), (tm * A_HEADS, A_V_DIM)
    if kv_prev is None:
        kv_specs = [pl.BlockSpec((depth,) + k_block, lambda i: (0,) + k_index(i)),
                    pl.BlockSpec((depth,) + v_block, lambda i: (0, i, 0))]
        alias_specs, alias_args, aliases = [], (), {}
    else:
        kv_specs = [pl.BlockSpec((None,) + k_block, lambda i: (layer,) + k_index(i)),
                    pl.BlockSpec((None,) + v_block, lambda i: (layer, i, 0))]
        alias_specs = [pl.BlockSpec(memory_space=pl.ANY)] * 2
        alias_args, aliases = tuple(kv_prev), {8: 0, 9: 1}
    return pl.pallas_call(
        functools.partial(_proj_kernel, k_transposed=k_transposed, layer=layer, n_alias=len(alias_args)),
        grid=(t // tm,),
        in_specs=[row(d), _pick(norm, layer, 2),
                  pl.BlockSpec((None, n_main, d), lambda i: (layer, 0, 0), pipeline_mode=pl.Buffered(1)),
                  pl.BlockSpec((None, GATE_RANK, d), lambda i: (layer, n_main // GATE_RANK, 0),
                               pipeline_mode=pl.Buffered(1)),
                  _pick(wgu, layer), _pick(bg, layer), row(LANES), row(LANES)] + alias_specs,
        out_specs=kv_specs + [row(A_WIDTH), row(A_WIDTH), row(A_WIDTH), col(A_HEADS * V_ROWS),
                              row(B_KEYS), row(B_KEYS), row(B_WIDTH), row(B_WIDTH), row(B_KEYS)],
        out_shape=[sds((depth,) + k_shape, F32), sds((depth,) + v_shape, F32),
                   sds((t, A_WIDTH), BF16), sds((t, A_WIDTH), BF16), sds((t, A_WIDTH), BF16),
                   sds((A_HEADS * V_ROWS, t), BF16),
                   sds((t, B_KEYS), F32), sds((t, B_KEYS), F32), sds((t, B_WIDTH), BF16),
                   sds((t, B_WIDTH), BF16), sds((t, B_KEYS), F32)],
        input_output_aliases=aliases,
        compiler_params=_params("parallel"),
        name="mixer_proj",
    )(x, norm, w_t, w_t, wgu, bg, cos, sin, *alias_args)


def _diff_lambda(lp, lam_init):
    a = jnp.sum(lp[0:1] * lp[1:2], axis=-1, keepdims=True)
    b = jnp.sum(lp[2:3] * lp[3:4], axis=-1, keepdims=True)
    return jnp.exp(a) - jnp.exp(b) + lam_init


def _subln(o, g, lam_init):
    return _rms(o, g, SUBLN_EPS) * (1.0 - lam_init)


def _attn_prompt_kernel(qi_ref, kg_ref, fl_ref, q1_ref, q2_ref, k_ref, vt_ref, lp_ref, g_ref,
                        o_ref, m_ref, acc_ref, st_ref, *, lam_init, variants):
    n = pl.program_id(0)
    slots, tk, tq = st_ref.shape
    group = k_ref.shape[0] // tk
    flags = fl_ref[n]
    n_maps = 2 * A_HEADS
    nt = (((1,), (1,)), ((), ()))

    @pl.when((flags & 1) != 0)
    def _():
        m_ref[...] = jnp.full(m_ref.shape, MASK_VALUE, F32)
        acc_ref[...] = jnp.zeros(acc_ref.shape, F32)

    def step(n_present, n_masked):
        penalty = {}
        if n_masked:
            n_chunks = tk // CHUNK
            assert n_chunks <= A_HEAD_DIM
            lane_k = lax.broadcasted_iota(jnp.int32, (tk, LANES), 1)
            chunk_k = lax.broadcasted_iota(jnp.int32, (tk, LANES), 0) // CHUNK
            lane_q = lax.broadcasted_iota(jnp.int32, (tq, LANES), 1)
            chunk_q = (qi_ref[n] * tq) // CHUNK + lax.broadcasted_iota(jnp.int32, (tq, LANES), 0) // CHUNK
            keep, onehot = [], []
            for c in range(2):
                off = A_HEAD_DIM * (1 - c)
                own = (lane_k < A_HEAD_DIM) if c == 0 else (lane_k >= A_HEAD_DIM)
                keep.append(jnp.where(own, 1.0, 0.0).astype(BF16))
                onehot.append(jnp.where(lane_k - off == chunk_k, 1.0, 0.0).astype(BF16))
                j = lane_q - off
                for s in range(n_present - n_masked, n_present):
                    first_chunk = ((kg_ref[n] * group + s) * tk) // CHUNK
                    hidden = jnp.where(j >= 0, jnp.where(j < n_chunks, first_chunk + j, -1), -1) > chunk_q
                    penalty[s, c] = jnp.where(hidden, MASK_VALUE, 0.0).astype(BF16)

        units = [(s, idx) for s in range(n_present) for idx in range(n_maps)]

        def scores(u):
            s, idx = units[u]
            hh, c = divmod(idx, 2)
            sl = slice(hh * LANES, (hh + 1) * LANES)
            k = k_ref[s * tk:(s + 1) * tk, sl]
            q = (q1_ref, q2_ref)[c][:, sl]
            if (s, c) in penalty:
                k = k * keep[c] + onehot[c]
                q = q + penalty[s, c]
            st_ref[u % slots] = lax.dot_general(k, q, nt, preferred_element_type=F32)

        def accumulate(u):
            s, idx = units[u]
            hh = idx // 2
            st = st_ref[u % slots]
            m_old = m_ref[idx]
            m_new = jnp.maximum(m_old, jnp.max(st, axis=0, keepdims=True))
            m_ref[idx] = m_new
            p = jnp.exp2(st - m_new).astype(BF16)
            pv = jnp.dot(vt_ref[hh * V_ROWS:(hh + 1) * V_ROWS, s * tk:(s + 1) * tk], p, preferred_element_type=F32)
            acc_ref[idx] = jnp.exp2(m_old - m_new) * acc_ref[idx] + pv

        ahead = slots - 1
        for u in range(min(ahead, len(units))):
            scores(u)
        for u in range(len(units)):
            if u + ahead < len(units):
                scores(u + ahead)
            accumulate(u)

    for v, (n_present, n_masked) in enumerate(variants):
        @pl.when((flags >> 2) == v)
        def _():
            step(n_present, n_masked)

    @pl.when((flags & 2) != 0)
    def _():
        lam = _diff_lambda(lp_ref[...], lam_init)
        for hh in range(A_HEADS):
            outs = []
            for idx in (2 * hh, 2 * hh + 1):
                outs.append(acc_ref[idx, 0:A_V_DIM, :] * (1.0 / acc_ref[idx, A_V_DIM:A_V_DIM + 1, :]))
            o = (outs[0] - lam * outs[1]).T
            o_ref[:, hh * LANES:(hh + 1) * LANES] = _subln(o, g_ref[...], lam_init).astype(BF16)


def _attn_steps(t, tq, tk, group):
    qi, kg, fl, variants = [], [], [], []
    for i in range(t // tq):
        q_lo, q_hi = (i * tq) // CHUNK, (i * tq + tq - 1) // CHUNK
        js = [j for j in range(t // tk) if (j * tk) // CHUNK <= q_hi]
        for g in range(0, len(js), group):
            members = js[g:g + group]
            variant = (len(members), sum((j * tk + tk - 1) // CHUNK > q_lo for j in members))
            if variant not in variants:
                variants.append(variant)
            qi.append(i)
            kg.append(g // group)
            fl.append((1 if g == 0 else 0) | (2 if g + group >= len(js) else 0) | (variants.index(variant) << 2))
    return (np.asarray(qi, np.int32), np.asarray(kg, np.int32), np.asarray(fl, np.int32)), tuple(variants)


def _attn_prompt(q1, q2, kb, vt, lam_p, subln_g, layer, lam_init):
    t = q1.shape[0]
    tq, tk = min(ATTN_Q_TILE, t), min(ATTN_K_TILE, t)
    group = ATTN_K_GROUP if t % (ATTN_K_GROUP * tk) == 0 else 1
    (qi, kg, fl), variants = _attn_steps(t, tq, tk, group)
    grid_spec = pltpu.PrefetchScalarGridSpec(
        num_scalar_prefetch=3,
        grid=(len(qi),),
        in_specs=[
            pl.BlockSpec((tq, A_WIDTH), lambda n, qi, kg, fl: (qi[n], 0)),
            pl.BlockSpec((tq, A_WIDTH), lambda n, qi, kg, fl: (qi[n], 0)),
            pl.BlockSpec((group * tk, A_WIDTH), lambda n, qi, kg, fl: (kg[n], 0)),
            pl.BlockSpec((A_HEADS * V_ROWS, group * tk), lambda n, qi, kg, fl: (0, kg[n])),
            _pick(lam_p, layer),
            _pick(subln_g, layer),
        ],
        out_specs=pl.BlockSpec((tq, A_WIDTH), lambda n, qi, kg, fl: (qi[n], 0)),
        scratch_shapes=[pltpu.VMEM((2 * A_HEADS, 1, tq), F32),
                        pltpu.VMEM((2 * A_HEADS, V_ROWS, tq), F32),
                        pltpu.VMEM((ATTN_SCORE_SLOTS, tk, tq), F32)],
    )
    return pl.pallas_call(
        functools.partial(_attn_prompt_kernel, lam_init=lam_init, variants=variants),
        grid_spec=grid_spec,
        out_shape=jax.ShapeDtypeStruct((t, A_WIDTH), BF16),
        compiler_params=_params("arbitrary"),
        name="attn_prompt",
    )(jnp.asarray(qi), jnp.asarray(kg), jnp.asarray(fl), q1, q2, kb, vt, lam_p, subln_g)


def _attn_sample_kernel(q1_ref, q2_ref, kn_ref, vn_ref, ckt_ref, cv_ref, lp_ref, g_ref, o_ref, *, lam_init):
    past = cv_ref.shape[0] // A_HEADS
    tq = q1_ref.shape[0]
    qpos = past + lax.broadcasted_iota(jnp.int32, (tq, past), 0)
    vis_past = (lax.broadcasted_iota(jnp.int32, (tq, past), 1) // CHUNK) <= (qpos // CHUNK)
    qpos_n = lax.broadcasted_iota(jnp.int32, (tq, tq), 0)
    vis_new = (lax.broadcasted_iota(jnp.int32, (tq, tq), 1) + past) // CHUNK <= (qpos_n + past) // CHUNK
    lam = _diff_lambda(lp_ref[...], lam_init)
    nt = (((1,), (1,)), ((), ()))
    scores = []
    for hh in range(A_HEADS):
        sl = slice(hh * LANES, (hh + 1) * LANES)
        ckt = ckt_ref[2 * hh:2 * hh + 2].reshape(2 * A_HEAD_DIM, past).astype(BF16)
        kn = kn_ref[:, sl]
        for q_ref in (q1_ref, q2_ref):
            q = q_ref[:, sl]
            scores.append((jnp.where(vis_past, jnp.dot(q, ckt, preferred_element_type=F32), MASK_VALUE),
                           jnp.where(vis_new, lax.dot_general(q, kn, nt, preferred_element_type=F32), MASK_VALUE)))
    probs = []
    for s_p, s_n in scores:
        m = jnp.maximum(jnp.max(s_p, axis=-1, keepdims=True), jnp.max(s_n, axis=-1, keepdims=True))
        p_p = jnp.exp2(s_p - m)
        p_n = jnp.exp2(s_n - m)
        l = jnp.sum(p_p, axis=-1, keepdims=True) + jnp.sum(p_n, axis=-1, keepdims=True)
        probs.append((p_p.astype(BF16), p_n.astype(BF16), 1.0 / l))
    for hh in range(A_HEADS):
        cv = cv_ref[pl.ds(hh, past, stride=A_HEADS), :].astype(BF16)
        vn = vn_ref[pl.ds(hh, tq, stride=A_HEADS), :].astype(BF16)
        outs = []
        for p_p, p_n, inv_l in probs[2 * hh:2 * hh + 2]:
            pv = jnp.dot(p_p, cv, preferred_element_type=F32) + jnp.dot(p_n, vn, preferred_element_type=F32)
            outs.append(pv * inv_l)
        o = outs[0] - lam * outs[1]
        o_ref[:, hh * LANES:(hh + 1) * LANES] = _subln(o, g_ref[...], lam_init).astype(BF16)


def _attn_sample(q1, q2, kb, vf, cache_kt, cache_v, layer, lam_p, subln_g, lam_init):
    _, nb, nh2, hd, past = cache_kt.shape
    tq = q1.shape[0] // nb
    new = lambda: pl.BlockSpec((tq, A_WIDTH), lambda b: (b, 0))
    return pl.pallas_call(
        functools.partial(_attn_sample_kernel, lam_init=lam_init),
        grid=(nb,),
        in_specs=[new(), new(), new(), pl.BlockSpec((None, tq * A_HEADS, A_V_DIM), lambda b: (layer, b, 0)),
                  pl.BlockSpec((None, None, nh2, hd, past), lambda b: (layer, b, 0, 0, 0)),
                  pl.BlockSpec((None, None, past * A_HEADS, A_V_DIM), lambda b: (layer, b, 0, 0)),
                  _pick(lam_p, layer), _pick(subln_g, layer)],
        out_specs=new(),
        out_shape=jax.ShapeDtypeStruct((nb * tq, A_WIDTH), BF16),
        compiler_params=_params("parallel"),
        name="attn_sample",
    )(q1, q2, kb, vf, cache_kt, cache_v, lam_p, subln_g)


def _gla_kernel(q_ref, k_ref, v_ref, la_ref, r_ref, s0_ref, g_ref, o_ref, s_ref, st_ref, *, chunk, carry):
    blk = pl.program_id(1)
    tokens = q_ref.shape[0]

    if carry:
        @pl.when(blk == 0)
        def _():
            st_ref[...] = s0_ref[...].T

    span = min(tokens, max(chunk, GLA_SPAN))
    row = lax.broadcasted_iota(jnp.int32, (span, span), 0)
    col = lax.broadcasted_iota(jnp.int32, (span, span), 1)
    same_chunk = (row // chunk) == (col // chunk)
    causal = jnp.where(same_chunk, col, span) <= row
    tri = jnp.where(causal, 1.0, 0.0).astype(BF16)
    ones = jnp.where(same_chunk, 1.0, 0.0).astype(BF16)
    lane = lax.broadcasted_iota(jnp.int32, (span, B_KEYS), 1)
    nt = (((1,), (1,)), ((), ()))
    tn = (((0,), (0,)), ((), ()))

    groups = []
    for g0 in range(0, tokens, span):
        rs = slice(g0, g0 + span)
        la = la_ref[rs, :]
        la_hi = la.astype(BF16)
        la_lo = (la - la_hi.astype(F32)).astype(BF16)
        b = (jnp.dot(tri, la_hi, preferred_element_type=F32)
             + jnp.dot(tri, la_lo, preferred_element_type=F32))
        total = (jnp.dot(ones, la_hi, preferred_element_type=F32)
                 + jnp.dot(ones, la_lo, preferred_element_type=F32))
        qe = q_ref[rs, :] * jnp.exp(b)
        k = k_ref[rs, :]
        ke = (k * jnp.exp(-b)).astype(BF16)
        kd = k * jnp.exp(total - b)
        qe_h, kd_h, o_intra = [], [], []
        for hh in range(B_HEADS):
            in_head = (lane // B_KEY_DIM) == hh
            qe_h.append(jnp.where(in_head, qe, 0.0).astype(BF16))
            kd_h.append(jnp.where(in_head, kd, 0.0).astype(BF16))
            att = lax.dot_general(qe_h[hh], ke, nt, preferred_element_type=F32)
            att = jnp.where(causal, att, 0.0).astype(BF16)
            o_intra.append(jnp.dot(att, v_ref[rs, hh * B_VAL_DIM:(hh + 1) * B_VAL_DIM], preferred_element_type=F32))
        groups.append((qe_h, kd_h, o_intra, total))

    n_chunks = tokens // chunk
    where = [(groups[c * chunk // span], slice(c * chunk % span, c * chunk % span + chunk)) for c in range(n_chunks)]
    increments = []
    for c, ((_, kd_h, _, _), rl) in enumerate(where):
        r = slice(c * chunk, (c + 1) * chunk)
        kd_stack = jnp.concatenate([kd_h[hh][rl] for hh in range(B_HEADS)], axis=0)
        v_stack = jnp.concatenate([v_ref[r, hh * B_VAL_DIM:(hh + 1) * B_VAL_DIM] for hh in range(B_HEADS)], axis=0)
        increments.append(lax.dot_general(v_stack, kd_stack, tn, preferred_element_type=F32))
    states = []
    st = st_ref[...] if carry else None
    for c, ((_, _, _, total), rl) in enumerate(where):
        if not carry:
            st = s0_ref[c].T
        states.append(st)
        st = st * jnp.exp(total[rl.start:rl.start + 1, :]) + increments[c]
        if not carry:
            s_ref[c] = st.T
    if carry:
        st_ref[...] = st
    for c, ((qe_h, _, o_intra, _), rl) in enumerate(where):
        r = slice(c * chunk, (c + 1) * chunk)
        q_stack = jnp.concatenate([qe_h[hh][rl] for hh in range(B_HEADS)], axis=0)
        o_inter = lax.dot_general(q_stack, states[c].astype(BF16), nt, preferred_element_type=F32)
        for hh in range(B_HEADS):
            vs = slice(hh * B_VAL_DIM, (hh + 1) * B_VAL_DIM)
            o = o_intra[hh][rl] + o_inter[hh * chunk:(hh + 1) * chunk]
            rg = r_ref[r, vs].astype(F32)
            o_ref[r, vs] = (_rms(o, g_ref[...], SUBLN_EPS) * (rg * jax.nn.sigmoid(rg))).astype(BF16)

    if carry:
        @pl.when(blk == pl.num_programs(1) - 1)
        def _():
            s_ref[...] = st_ref[...].T


def _gla(q, k, v, la, rg, s0, s0_layer, g, layer, chunk):
    nb = s0.shape[1]
    t = q.shape[0] // nb
    carry = t > chunk
    if carry:
        tb = min(GLA_BLOCK, t)
        grid = (nb, t // tb)
        state = pl.BlockSpec((None, B_KEYS, B_VAL_DIM), lambda b, i: (b, 0, 0))
        state_in = pl.BlockSpec((None, None, B_KEYS, B_VAL_DIM), lambda b, i: (s0_layer, b, 0, 0))
    else:
        assert t == chunk
        streams = math.gcd(nb, max(1, GLA_BLOCK // t))
        tb = streams * t
        grid = (nb // streams, 1)
        state = pl.BlockSpec((streams, B_KEYS, B_VAL_DIM), lambda b, i: (b, 0, 0))
        state_in = pl.BlockSpec((None, streams, B_KEYS, B_VAL_DIM), lambda b, i: (s0_layer, b, 0, 0))
    nblk = grid[1]
    tok = lambda n: pl.BlockSpec((tb, n), lambda b, i: (b * nblk + i, 0))
    return pl.pallas_call(
        functools.partial(_gla_kernel, chunk=chunk, carry=carry),
        grid=grid,
        in_specs=[tok(B_KEYS), tok(B_KEYS), tok(B_WIDTH), tok(B_KEYS), tok(B_WIDTH), state_in, _pick(g, layer)],
        out_specs=[tok(B_WIDTH), state],
        out_shape=[jax.ShapeDtypeStruct((nb * t, B_WIDTH), BF16),
                   jax.ShapeDtypeStruct((nb, B_KEYS, B_VAL_DIM), F32)],
        scratch_shapes=[pltpu.VMEM((B_VAL_DIM, B_KEYS), F32)],
        compiler_params=_params("parallel", "arbitrary"),
        name="gla",
    )(q, k, v, la, rg, s0, g)


def _merge_kernel(x_ref, oa_ref, ob_ref, gpre_ref, gpost_ref, wgab_ref, woa_ref, wob_ref, wo_ref, o_ref):
    tm, d = x_ref.shape
    halves = [slice(i * (tm // MERGE_SPLIT), (i + 1) * (tm // MERGE_SPLIT)) for i in range(MERGE_SPLIT)]
    nt = (((1,), (1,)), ((), ()))
    first = []
    for r in halves:
        h = _rms(x_ref[r, :], gpre_ref[...], NORM_EPS).astype(BF16)
        first.append((lax.dot_general(h, wgab_ref[0], nt, preferred_element_type=F32),
                      jnp.dot(oa_ref[r, :], woa_ref[...], preferred_element_type=F32),
                      jnp.dot(ob_ref[r, :], wob_ref[...], preferred_element_type=F32)))
    mixed = []
    for gab, ya, yb in first:
        m = jax.nn.sigmoid(gab[:, :d]) * ya + jax.nn.sigmoid(gab[:, d:]) * yb
        mixed.append(jnp.dot(m.astype(BF16), wo_ref[...], preferred_element_type=F32))
    for r, y in zip(halves, mixed):
        o_ref[r, :] = x_ref[r, :] + _rms(y, gpost_ref[...], NORM_EPS)


def _merge(x, oa, ob, norm, w_t, woa, wob, wo, layer):
    t, d = x.shape
    tm = min(MERGE_ROW_TILE, t)
    row = lambda n: pl.BlockSpec((tm, n), lambda i: (i, 0))
    gate_rows = w_t.shape[1] - 2 * d
    w_gates = pl.BlockSpec((pl.Element(1), pl.Element(2 * d), pl.Element(d)), lambda i: (layer, gate_rows, 0),
                           pipeline_mode=pl.Buffered(1))
    return pl.pallas_call(
        _merge_kernel,
        grid=(t // tm,),
        in_specs=[row(d), row(A_WIDTH), row(B_WIDTH), _pick(norm, layer, 2), _pick(norm, layer, 3),
                  w_gates, _pick(woa, layer), _pick(wob, layer), _pick(wo, layer)],
        out_specs=row(d),
        out_shape=jax.ShapeDtypeStruct((t, d), F32),
        compiler_params=_params("parallel"),
        name="mixer_merge",
    )(x, oa, ob, norm, norm, w_t, woa, wob, wo)


def _rope_tables(pos):
    half = A_HEAD_DIM // 2
    inv = jnp.exp(-math.log(ROPE_THETA) * jnp.arange(half, dtype=F32) / half)
    ang = pos.astype(F32)[:, None] * inv[None, :]
    cos, sin = jnp.cos(ang), jnp.sin(ang)
    reps = LANES // A_HEAD_DIM
    return (jnp.tile(jnp.concatenate([cos, cos], axis=-1), (1, reps)),
            jnp.tile(jnp.concatenate([-sin, sin], axis=-1), (1, reps)))


def kernel(x_prompt, x_sample, cache_k, cache_v, state_gla, norm_g, ffn_w_gate, ffn_w_up, ffn_w_down, w_in,
           w_gate_up, b_gate, lambda_p, subln_g, gla_norm_g, w_out_a, w_out_b, w_out):
    depth = w_in.shape[0]
    bp, tp, d = x_prompt.shape
    bs, ts, _ = x_sample.shape
    past = cache_k.shape[2]
    assert bp == 1 and tp % CHUNK == 0

    n_qk = 2 * A_HEADS * A_HEAD_DIM
    assert w_in.shape[2] == 2 * n_qk + A_WIDTH + 2 * B_KEYS + 2 * B_WIDTH + GATE_RANK + 2 * d

    norm = norm_g.reshape(depth, norm_g.shape[1], 1, d)
    wg, wu, wd = ffn_w_gate.astype(BF16), ffn_w_up.astype(BF16), ffn_w_down.astype(BF16)
    w_t = jnp.transpose(w_in, (0, 2, 1)).astype(BF16)
    w_gu = w_gate_up.astype(BF16)
    bg = b_gate[:, None, :]
    woa, wob, wo = w_out_a.astype(BF16), w_out_b.astype(BF16), w_out.astype(BF16)
    sg = subln_g[:, None, :]
    gg = gla_norm_g[:, None, :]

    cos_p, sin_p = _rope_tables(jnp.arange(tp))
    cos_s, sin_s = _rope_tables(jnp.tile(past + jnp.arange(ts), bs))

    cache_kt = jnp.transpose(cache_k, (0, 1, 3, 4, 2))
    cache_vr = cache_v.reshape(depth, bs, past * A_HEADS, A_V_DIM)
    s0_sample = state_gla.reshape(depth, bs, B_KEYS, B_VAL_DIM)
    s0_prompt = jnp.zeros((1, bp, B_KEYS, B_VAL_DIM), F32)

    xp = x_prompt.reshape(tp, d)
    xs = x_sample.reshape(bs * ts, d)
    kv_p = kv_s = None
    states_p, states_s = [], []
    for l in range(depth):
        lam_init = 0.8 - 0.6 * math.exp(-0.3 * l)

        def layer(x, cos, sin, chunk, s0, s0_layer, kv, is_prompt):
            x = _ffn(x, norm, wg, wu, wd, l, 0)
            kf, vf, q1, q2, kb, vt, qg, kg, vg, rg, la = _proj(x, norm, w_t, w_gu, bg, cos, sin, l, is_prompt, kv)
            if is_prompt:
                oa = _attn_prompt(q1, q2, kb, vt, lambda_p, sg, l, lam_init)
            else:
                oa = _attn_sample(q1, q2, kb, vf, cache_kt, cache_vr, l, lambda_p, sg, lam_init)
            ob, s_new = _gla(qg, kg, vg, la, rg, s0, s0_layer, gg, l, chunk)
            x = _merge(x, oa, ob, norm, w_t, woa, wob, wo, l)
            x = _ffn(x, norm, wg, wu, wd, l, 1)
            return x, (kf, vf), s_new

        xp, kv_p, sp = layer(xp, cos_p, sin_p, CHUNK, s0_prompt, 0, kv_p, True)
        xs, kv_s, sn = layer(xs, cos_s, sin_s, ts, s0_sample, l, kv_s, False)
        states_p.append(sp)
        states_s.append(sn)

    k_prompt = jnp.transpose(kv_p[0].reshape(depth, bp, 2 * A_HEADS, A_HEAD_DIM, tp), (0, 1, 4, 2, 3))
    v_prompt = kv_p[1].reshape(depth, bp, tp, A_HEADS, A_V_DIM)
    s_prompt = jnp.stack(states_p).reshape(depth, bp, B_HEADS, B_KEY_DIM, B_VAL_DIM)
    k_sample = kv_s[0].reshape(depth, bs, ts, 2 * A_HEADS, A_HEAD_DIM)
    v_sample = kv_s[1].reshape(depth, bs, ts, A_HEADS, A_V_DIM)
    s_sample = jnp.stack(states_s).reshape(depth, bs, B_HEADS, B_KEY_DIM, B_VAL_DIM)
    return (xp.reshape(bp, tp, d), xs.reshape(bs, ts, d), k_prompt, v_prompt, s_prompt,
            k_sample, v_sample, s_sample)
```

```python
import functools
import math

import jax
import jax.numpy as jnp
import numpy as np
from jax import lax
from jax.experimental import pallas as pl
from jax.experimental.pallas import tpu as pltpu

F32 = jnp.float32
BF16 = jnp.bfloat16

CHUNK = 64
A_HEADS = 4
A_HEAD_DIM = 64
A_V_DIM = 2 * A_HEAD_DIM
A_WIDTH = A_HEADS * A_V_DIM
B_HEADS = 4
B_KEY_DIM = 64
B_VAL_DIM = 128
B_KEYS = B_HEADS * B_KEY_DIM
B_WIDTH = B_HEADS * B_VAL_DIM
GATE_RANK = 16
GATE_TAU = 16.0
ROPE_THETA = 10000.0
NORM_EPS = 1e-6
SUBLN_EPS = 1e-5
LANES = 128
MASK_VALUE = -1e30

ROW_TILE = 512
FFN_ROW_TILE = 1024
ROW_GROUP = 256
MERGE_ROW_TILE = 1024
MERGE_SPLIT = 2
ATTN_Q_TILE = 512
ATTN_K_TILE = 512
ATTN_K_GROUP = 4
ATTN_SCORE_SLOTS = 3
V_ROWS = A_V_DIM + 16
LOG2E = math.log2(math.e)
GLA_BLOCK = 1024
GLA_SPAN = 256
VMEM_LIMIT = 56 * 1024 * 1024


def _rms(x, g, eps):
    return x * lax.rsqrt(jnp.mean(x * x, axis=-1, keepdims=True) + eps) * g


def _params(*sem):
    return pltpu.CompilerParams(dimension_semantics=sem, vmem_limit_bytes=VMEM_LIMIT)


def _row_groups(rows, group):
    return [slice(i, i + group) for i in range(0, rows, group)]


def _ffn_kernel(x_ref, gpre_ref, gpost_ref, wg_ref, wu_ref, wd_ref, o_ref):
    groups = _row_groups(x_ref.shape[0], min(x_ref.shape[0], ROW_GROUP))
    gate_up = []
    for r in groups:
        h = _rms(x_ref[r, :], gpre_ref[...], NORM_EPS).astype(BF16)
        gate_up.append((jnp.dot(h, wg_ref[...], preferred_element_type=F32),
                        jnp.dot(h, wu_ref[...], preferred_element_type=F32)))
    down = []
    for g, u in gate_up:
        a = (g * jax.nn.sigmoid(g) * u).astype(BF16)
        down.append(jnp.dot(a, wd_ref[...], preferred_element_type=F32))
    for r, y in zip(groups, down):
        o_ref[r, :] = x_ref[r, :] + 0.5 * _rms(y, gpost_ref[...], NORM_EPS)


def _pick(a, *lead):
    rest = a.shape[len(lead):]
    index = tuple(lead) + (0,) * len(rest)
    return pl.BlockSpec((None,) * len(lead) + rest, lambda *_: index, pipeline_mode=pl.Buffered(1))


def _ffn(x, norm, wg, wu, wd, layer, which):
    t, d = x.shape
    tm = min(FFN_ROW_TILE, t)
    row = pl.BlockSpec((tm, d), lambda i: (i, 0))
    return pl.pallas_call(
        _ffn_kernel,
        grid=(t // tm,),
        in_specs=[row, _pick(norm, layer, 4 * which), _pick(norm, layer, 4 * which + 1),
                  _pick(wg, layer, which), _pick(wu, layer, which), _pick(wd, layer, which)],
        out_specs=row,
        out_shape=jax.ShapeDtypeStruct((t, d), F32),
        compiler_params=_params("parallel"),
        name="ffn",
    )(x, norm, norm, wg, wu, wd)


def _rope(z, cos, sin_signed, first_half):
    rot = jnp.where(first_half, pltpu.roll(z, LANES - A_HEAD_DIM // 2, 1), pltpu.roll(z, A_HEAD_DIM // 2, 1))
    return z * cos + rot * sin_signed


def _proj_kernel(x_ref, g_ref, w_ref, wc_ref, wgu_ref, bg_ref, cos_ref, sin_ref, *rest, k_transposed, layer, n_alias):
    (kf_ref, vf_ref, q1_ref, q2_ref, kb_ref, vt_ref, qg_ref, kg_ref, vg_ref, rg_ref, la_ref) = rest[n_alias:]
    nt = (((1,), (1,)), ((), ()))
    if n_alias == 0:
        for j in range(kf_ref.shape[0]):
            if j != layer:
                kf_ref[j] = jnp.zeros(kf_ref.shape[1:], F32)
                vf_ref[j] = jnp.zeros(vf_ref.shape[1:], F32)
        kf_ref, vf_ref = kf_ref.at[layer], vf_ref.at[layer]
    h = _rms(x_ref[...], g_ref[...], NORM_EPS).astype(BF16)
    n_att = 4 * A_HEADS * A_HEAD_DIM + A_WIDTH
    code = lax.dot_general(h, wc_ref[...], nt, preferred_element_type=F32).astype(BF16)
    z = lax.dot_general(h, w_ref[0:n_att, :], nt, preferred_element_type=F32)
    gate = jnp.dot(code, wgu_ref[...], preferred_element_type=F32) + bg_ref[...]
    la_ref[...] = (jnp.minimum(gate, 0.0) - jnp.log1p(jnp.exp(-jnp.abs(gate)))) * (1.0 / GATE_TAU)
    zg = lax.dot_general(h, w_ref[n_att:, :], nt, preferred_element_type=F32)
    cos = cos_ref[...]
    sin = sin_ref[...]
    lane = lax.broadcasted_iota(jnp.int32, cos.shape, 1)
    first_half = (lane % A_HEAD_DIM) < (A_HEAD_DIM // 2)
    even_head = lane < A_HEAD_DIM
    scale = A_HEAD_DIM ** -0.5 * LOG2E
    o = 0
    for hh in range(A_HEADS):
        sl = slice(hh * LANES, (hh + 1) * LANES)
        q = _rope(z[:, o + hh * LANES:o + (hh + 1) * LANES], cos, sin, first_half) * scale
        q1_ref[:, sl] = jnp.where(even_head, q, 0.0).astype(BF16)
        q2_ref[:, sl] = jnp.where(even_head, 0.0, q).astype(BF16)
    o += 2 * A_HEADS * A_HEAD_DIM
    for hh in range(A_HEADS):
        sl = slice(hh * LANES, (hh + 1) * LANES)
        k = _rope(z[:, o + hh * LANES:o + (hh + 1) * LANES], cos, sin, first_half)
        if k_transposed:
            kf_ref[sl, :] = k.T
        else:
            kf_ref[:, sl] = k
        kb_ref[:, sl] = k.astype(BF16)
    o += 2 * A_HEADS * A_HEAD_DIM
    va = z[:, o:o + A_WIDTH]
    for hh in range(A_HEADS):
        vf_ref[pl.ds(hh, va.shape[0], stride=A_HEADS), :] = va[:, hh * A_V_DIM:(hh + 1) * A_V_DIM]
        vt_ref[hh * V_ROWS:hh * V_ROWS + A_V_DIM, :] = va[:, hh * A_V_DIM:(hh + 1) * A_V_DIM].T.astype(BF16)
        vt_ref[hh * V_ROWS + A_V_DIM:(hh + 1) * V_ROWS, :] = jnp.ones((V_ROWS - A_V_DIM, va.shape[0]), BF16)
    o = 0
    qg_ref[...] = zg[:, o:o + B_KEYS] * (B_KEY_DIM ** -0.5)
    o += B_KEYS
    kg_ref[...] = zg[:, o:o + B_KEYS]
    o += B_KEYS
    vg_ref[...] = zg[:, o:o + B_WIDTH].astype(BF16)
    o += B_WIDTH
    rg_ref[...] = zg[:, o:o + B_WIDTH].astype(BF16)


def _proj(x, norm, w_t, wgu, bg, cos, sin, layer, k_transposed, kv_prev):
    t, d = x.shape
    depth = w_t.shape[0]
    tm = min(ROW_TILE, t)
    row = lambda n: pl.BlockSpec((tm, n), lambda i: (i, 0))
    col = lambda n: pl.BlockSpec((n, tm), lambda i: (0, i))
    sds = jax.ShapeDtypeStruct
    n_qk = 2 * A_HEADS * A_HEAD_DIM
    n_main = 2 * n_qk + A_WIDTH + 2 * B_KEYS + 2 * B_WIDTH
    assert n_main % GATE_RANK == 0
    k_shape = (n_qk, t) if k_transposed else (t, n_qk)
    k_block = (n_qk, tm) if k_transposed else (tm, n_qk)
    k_index = (lambda i: (0, i)) if k_transposed else (lambda i: (i, 0))
    v_shape, v_block = (t * A_HEADS, A_V_DIM), (tm * A_HEADS, A_V_DIM)
    if kv_prev is None:
        kv_specs = [pl.BlockSpec((depth,) + k_block, lambda i: (0,) + k_index(i)),
                    pl.BlockSpec((depth,) + v_block, lambda i: (0, i, 0))]
        alias_specs, alias_args, aliases = [], (), {}
    else:
        kv_specs = [pl.BlockSpec((None,) + k_block, lambda i: (layer,) + k_index(i)),
                    pl.BlockSpec((None,) + v_block, lambda i: (layer, i, 0))]
        alias_specs = [pl.BlockSpec(memory_space=pl.ANY)] * 2
        alias_args, aliases = tuple(kv_prev), {8: 0, 9: 1}
    return pl.pallas_call(
        functools.partial(_proj_kernel, k_transposed=k_transposed, layer=layer, n_alias=len(alias_args)),
        grid=(t // tm,),
        in_specs=[row(d), _pick(norm, layer, 2),
                  pl.BlockSpec((None, n_main, d), lambda i: (layer, 0, 0), pipeline_mode=pl.Buffered(1)),
                  pl.BlockSpec((None, GATE_RANK, d), lambda i: (layer, n_main // GATE_RANK, 0),
                               pipeline_mode=pl.Buffered(1)),
                  _pick(wgu, layer), _pick(bg, layer), row(LANES), row(LANES)] + alias_specs,
        out_specs=kv_specs + [row(A_WIDTH), row(A_WIDTH), row(A_WIDTH), col(A_HEADS * V_ROWS),
                              row(B_KEYS), row(B_KEYS), row(B_WIDTH), row(B_WIDTH), row(B_KEYS)],
        out_shape=[sds((depth,) + k_shape, F32), sds((depth,) + v_shape, F32),
                   sds((t, A_WIDTH), BF16), sds((t, A_WIDTH), BF16), sds((t, A_WIDTH), BF16),
                   sds((A_HEADS * V_ROWS, t), BF16),
                   sds((t, B_KEYS), F32), sds((t, B_KEYS), F32), sds((t, B_WIDTH), BF16),
                   sds((t, B_WIDTH), BF16), sds((t, B_KEYS), F32)],
        input_output_aliases=aliases,
        compiler_params=_params("parallel"),
        name="mixer_proj",
    )(x, norm, w_t, w_t, wgu, bg, cos, sin, *alias_args)


def _diff_lambda(lp, lam_init):
    a = jnp.sum(lp[0:1] * lp[1:2], axis=-1, keepdims=True)
    b = jnp.sum(lp[2:3] * lp[3:4], axis=-1, keepdims=True)
    return jnp.exp(a) - jnp.exp(b) + lam_init


def _subln(o, g, lam_init):
    return _rms(o, g, SUBLN_EPS) * (1.0 - lam_init)


def _attn_prompt_kernel(qi_ref, kg_ref, fl_ref, q1_ref, q2_ref, k_ref, vt_ref, lp_ref, g_ref,
                        o_ref, m_ref, acc_ref, st_ref, *, lam_init, variants):
    n = pl.program_id(0)
    slots, tk, tq = st_ref.shape
    group = k_ref.shape[0] // tk
    flags = fl_ref[n]
    n_maps = 2 * A_HEADS
    nt = (((1,), (1,)), ((), ()))

    @pl.when((flags & 1) != 0)
    def _():
        m_ref[...] = jnp.full(m_ref.shape, MASK_VALUE, F32)
        acc_ref[...] = jnp.zeros(acc_ref.shape, F32)

    def step(n_present, n_masked, last):
        penalty = {}
        if n_masked:
            n_chunks = tk // CHUNK
            assert n_chunks <= A_HEAD_DIM
            lane_k = lax.broadcasted_iota(jnp.int32, (tk, LANES), 1)
            chunk_k = lax.broadcasted_iota(jnp.int32, (tk, LANES), 0) // CHUNK
            lane_q = lax.broadcasted_iota(jnp.int32, (tq, LANES), 1)
            chunk_q = (qi_ref[n] * tq) // CHUNK + lax.broadcasted_iota(jnp.int32, (tq, LANES), 0) // CHUNK
            keep, onehot = [], []
            for c in range(2):
                off = A_HEAD_DIM * (1 - c)
                own = (lane_k < A_HEAD_DIM) if c == 0 else (lane_k >= A_HEAD_DIM)
                keep.append(jnp.where(own, 1.0, 0.0).astype(BF16))
                onehot.append(jnp.where(lane_k - off == chunk_k, 1.0, 0.0).astype(BF16))
                j = lane_q - off
                for s in range(n_present - n_masked, n_present):
                    first_chunk = ((kg_ref[n] * group + s) * tk) // CHUNK
                    hidden = jnp.where(j >= 0, jnp.where(j < n_chunks, first_chunk + j, -1), -1) > chunk_q
                    penalty[s, c] = jnp.where(hidden, MASK_VALUE, 0.0).astype(BF16)

        units = [(s, idx) for s in range(n_present) for idx in range(n_maps)]

        def scores(u):
            s, idx = units[u]
            hh, c = divmod(idx, 2)
            sl = slice(hh * LANES, (hh + 1) * LANES)
            k = k_ref[s * tk:(s + 1) * tk, sl]
            q = (q1_ref, q2_ref)[c][:, sl]
            if (s, c) in penalty:
                k = k * keep[c] + onehot[c]
                q = q + penalty[s, c]
            st_ref[u % slots] = lax.dot_general(k, q, nt, preferred_element_type=F32)

        def accumulate(u):
            s, idx = units[u]
            hh = idx // 2
            st = st_ref[u % slots]
            m_old = m_ref[idx]
            m_new = jnp.maximum(m_old, jnp.max(st, axis=0, keepdims=True))
            m_ref[idx] = m_new
            p = jnp.exp2(st - m_new).astype(BF16)
            pv = jnp.dot(vt_ref[hh * V_ROWS:(hh + 1) * V_ROWS, s * tk:(s + 1) * tk], p, preferred_element_type=F32)
            acc_ref[idx] = jnp.exp2(m_old - m_new) * acc_ref[idx] + pv

        def finish(hh):
            outs = []
            for idx in (2 * hh, 2 * hh + 1):
                outs.append(acc_ref[idx, 0:A_V_DIM, :] * (1.0 / acc_ref[idx, A_V_DIM:A_V_DIM + 1, :]))
            o = (outs[0] - _diff_lambda(lp_ref[...], lam_init) * outs[1]).T
            o_ref[:, hh * LANES:(hh + 1) * LANES] = _subln(o, g_ref[...], lam_init).astype(BF16)

        ahead = slots - 1
        for u in range(min(ahead, len(units))):
            scores(u)
        for u in range(len(units)):
            if u + ahead < len(units):
                scores(u + ahead)
            accumulate(u)
            s, idx = units[u]
            if last and s == n_present - 1 and idx % 2 == 1:
                finish(idx // 2)

    for v, (n_present, n_masked, last) in enumerate(variants):
        @pl.when((flags >> 1) == v)
        def _():
            step(n_present, n_masked, last)


def _attn_steps(t, tq, tk, group):
    qi, kg, fl, variants = [], [], [], []
    for i in range(t // tq):
        q_lo, q_hi = (i * tq) // CHUNK, (i * tq + tq - 1) // CHUNK
        js = [j for j in range(t // tk) if (j * tk) // CHUNK <= q_hi]
        for g in range(0, len(js), group):
            members = js[g:g + group]
            variant = (len(members), sum((j * tk + tk - 1) // CHUNK > q_lo for j in members), g + group >= len(js))
            if variant not in variants:
                variants.append(variant)
            qi.append(i)
            kg.append(g // group)
            fl.append((1 if g == 0 else 0) | (variants.index(variant) << 1))
    return (np.asarray(qi, np.int32), np.asarray(kg, np.int32), np.asarray(fl, np.int32)), tuple(variants)


def _attn_prompt(q1, q2, kb, vt, lam_p, subln_g, layer, lam_init):
    t = q1.shape[0]
    tq, tk = min(ATTN_Q_TILE, t), min(ATTN_K_TILE, t)
    group = ATTN_K_GROUP if t % (ATTN_K_GROUP * tk) == 0 else 1
    (qi, kg, fl), variants = _attn_steps(t, tq, tk, group)
    grid_spec = pltpu.PrefetchScalarGridSpec(
        num_scalar_prefetch=3,
        grid=(len(qi),),
        in_specs=[
            pl.BlockSpec((tq, A_WIDTH), lambda n, qi, kg, fl: (qi[n], 0)),
            pl.BlockSpec((tq, A_WIDTH), lambda n, qi, kg, fl: (qi[n], 0)),
            pl.BlockSpec((group * tk, A_WIDTH), lambda n, qi, kg, fl: (kg[n], 0)),
            pl.BlockSpec((A_HEADS * V_ROWS, group * tk), lambda n, qi, kg, fl: (0, kg[n])),
            _pick(lam_p, layer),
            _pick(subln_g, layer),
        ],
        out_specs=pl.BlockSpec((tq, A_WIDTH), lambda n, qi, kg, fl: (qi[n], 0)),
        scratch_shapes=[pltpu.VMEM((2 * A_HEADS, 1, tq), F32),
                        pltpu.VMEM((2 * A_HEADS, V_ROWS, tq), F32),
                        pltpu.VMEM((ATTN_SCORE_SLOTS, tk, tq), F32)],
    )
    return pl.pallas_call(
        functools.partial(_attn_prompt_kernel, lam_init=lam_init, variants=variants),
        grid_spec=grid_spec,
        out_shape=jax.ShapeDtypeStruct((t, A_WIDTH), BF16),
        compiler_params=_params("arbitrary"),
        name="attn_prompt",
    )(jnp.asarray(qi), jnp.asarray(kg), jnp.asarray(fl), q1, q2, kb, vt, lam_p, subln_g)


def _attn_sample_kernel(q1_ref, q2_ref, kn_ref, vn_ref, ckt_ref, cv_ref, lp_ref, g_ref, o_ref, *, lam_init):
    past = cv_ref.shape[0] // A_HEADS
    tq = q1_ref.shape[0]
    qpos = past + lax.broadcasted_iota(jnp.int32, (tq, past), 0)
    vis_past = (lax.broadcasted_iota(jnp.int32, (tq, past), 1) // CHUNK) <= (qpos // CHUNK)
    qpos_n = lax.broadcasted_iota(jnp.int32, (tq, tq), 0)
    vis_new = (lax.broadcasted_iota(jnp.int32, (tq, tq), 1) + past) // CHUNK <= (qpos_n + past) // CHUNK
    lam = _diff_lambda(lp_ref[...], lam_init)
    nt = (((1,), (1,)), ((), ()))
    scores = []
    for hh in range(A_HEADS):
        sl = slice(hh * LANES, (hh + 1) * LANES)
        ckt = ckt_ref[2 * hh:2 * hh + 2].reshape(2 * A_HEAD_DIM, past).astype(BF16)
        kn = kn_ref[:, sl]
        for q_ref in (q1_ref, q2_ref):
            q = q_ref[:, sl]
            scores.append((jnp.where(vis_past, jnp.dot(q, ckt, preferred_element_type=F32), MASK_VALUE),
                           jnp.where(vis_new, lax.dot_general(q, kn, nt, preferred_element_type=F32), MASK_VALUE)))
    probs = []
    for s_p, s_n in scores:
        m = jnp.maximum(jnp.max(s_p, axis=-1, keepdims=True), jnp.max(s_n, axis=-1, keepdims=True))
        p_p = jnp.exp2(s_p - m)
        p_n = jnp.exp2(s_n - m)
        l = jnp.sum(p_p, axis=-1, keepdims=True) + jnp.sum(p_n, axis=-1, keepdims=True)
        probs.append((p_p.astype(BF16), p_n.astype(BF16), 1.0 / l))
    for hh in range(A_HEADS):
        cv = cv_ref[pl.ds(hh, past, stride=A_HEADS), :].astype(BF16)
        vn = vn_ref[pl.ds(hh, tq, stride=A_HEADS), :].astype(BF16)
        outs = []
        for p_p, p_n, inv_l in probs[2 * hh:2 * hh + 2]:
            pv = jnp.dot(p_p, cv, preferred_element_type=F32) + jnp.dot(p_n, vn, preferred_element_type=F32)
            outs.append(pv * inv_l)
        o = outs[0] - lam * outs[1]
        o_ref[:, hh * LANES:(hh + 1) * LANES] = _subln(o, g_ref[...], lam_init).astype(BF16)


def _attn_sample(q1, q2, kb, vf, cache_kt, cache_v, layer, lam_p, subln_g, lam_init):
    _, nb, nh2, hd, past = cache_kt.shape
    tq = q1.shape[0] // nb
    new = lambda: pl.BlockSpec((tq, A_WIDTH), lambda b: (b, 0))
    return pl.pallas_call(
        functools.partial(_attn_sample_kernel, lam_init=lam_init),
        grid=(nb,),
        in_specs=[new(), new(), new(), pl.BlockSpec((None, tq * A_HEADS, A_V_DIM), lambda b: (layer, b, 0)),
                  pl.BlockSpec((None, None, nh2, hd, past), lambda b: (layer, b, 0, 0, 0)),
                  pl.BlockSpec((None, None, past * A_HEADS, A_V_DIM), lambda b: (layer, b, 0, 0)),
                  _pick(lam_p, layer), _pick(subln_g, layer)],
        out_specs=new(),
        out_shape=jax.ShapeDtypeStruct((nb * tq, A_WIDTH), BF16),
        compiler_params=_params("parallel"),
        name="attn_sample",
    )(q1, q2, kb, vf, cache_kt, cache_v, lam_p, subln_g)


def _gla_kernel(q_ref, k_ref, v_ref, la_ref, r_ref, s0_ref, g_ref, o_ref, s_ref, st_ref, *, chunk, carry):
    blk = pl.program_id(1)
    tokens = q_ref.shape[0]

    if carry:
        @pl.when(blk == 0)
        def _():
            st_ref[...] = s0_ref[...].T

    span = min(tokens, max(chunk, GLA_SPAN))
    row = lax.broadcasted_iota(jnp.int32, (span, span), 0)
    col = lax.broadcasted_iota(jnp.int32, (span, span), 1)
    same_chunk = (row // chunk) == (col // chunk)
    causal = jnp.where(same_chunk, col, span) <= row
    tri = jnp.where(causal, 1.0, 0.0).astype(BF16)
    ones = jnp.where(same_chunk, 1.0, 0.0).astype(BF16)
    lane = lax.broadcasted_iota(jnp.int32, (span, B_KEYS), 1)
    nt = (((1,), (1,)), ((), ()))
    tn = (((0,), (0,)), ((), ()))

    groups = []
    for g0 in range(0, tokens, span):
        rs = slice(g0, g0 + span)
        la = la_ref[rs, :]
        la_hi = la.astype(BF16)
        la_lo = (la - la_hi.astype(F32)).astype(BF16)
        b = (jnp.dot(tri, la_hi, preferred_element_type=F32)
             + jnp.dot(tri, la_lo, preferred_element_type=F32))
        total = (jnp.dot(ones, la_hi, preferred_element_type=F32)
                 + jnp.dot(ones, la_lo, preferred_element_type=F32))
        qe = q_ref[rs, :] * jnp.exp(b)
        k = k_ref[rs, :]
        ke = (k * jnp.exp(-b)).astype(BF16)
        kd = k * jnp.exp(total - b)
        qe_h, kd_h, o_intra = [], [], []
        for hh in range(B_HEADS):
            in_head = (lane // B_KEY_DIM) == hh
            qe_h.append(jnp.where(in_head, qe, 0.0).astype(BF16))
            kd_h.append(jnp.where(in_head, kd, 0.0).astype(BF16))
            att = lax.dot_general(qe_h[hh], ke, nt, preferred_element_type=F32)
            att = jnp.where(causal, att, 0.0).astype(BF16)
            o_intra.append(jnp.dot(att, v_ref[rs, hh * B_VAL_DIM:(hh + 1) * B_VAL_DIM], preferred_element_type=F32))
        groups.append((qe_h, kd_h, o_intra, total))

    n_chunks = tokens // chunk
    where = [(groups[c * chunk // span], slice(c * chunk % span, c * chunk % span + chunk)) for c in range(n_chunks)]
    increments = []
    for c, ((_, kd_h, _, _), rl) in enumerate(where):
        r = slice(c * chunk, (c + 1) * chunk)
        kd_stack = jnp.concatenate([kd_h[hh][rl] for hh in range(B_HEADS)], axis=0)
        v_stack = jnp.concatenate([v_ref[r, hh * B_VAL_DIM:(hh + 1) * B_VAL_DIM] for hh in range(B_HEADS)], axis=0)
        increments.append(lax.dot_general(v_stack, kd_stack, tn, preferred_element_type=F32))
    states = []
    st = st_ref[...] if carry else None
    for c, ((_, _, _, total), rl) in enumerate(where):
        if not carry:
            st = s0_ref[c].T
        states.append(st)
        st = st * jnp.exp(total[rl.start:rl.start + 1, :]) + increments[c]
        if not carry:
            s_ref[c] = st.T
    if carry:
        st_ref[...] = st
    for c, ((qe_h, _, o_intra, _), rl) in enumerate(where):
        r = slice(c * chunk, (c + 1) * chunk)
        q_stack = jnp.concatenate([qe_h[hh][rl] for hh in range(B_HEADS)], axis=0)
        o_inter = lax.dot_general(q_stack, states[c].astype(BF16), nt, preferred_element_type=F32)
        for hh in range(B_HEADS):
            vs = slice(hh * B_VAL_DIM, (hh + 1) * B_VAL_DIM)
            o = o_intra[hh][rl] + o_inter[hh * chunk:(hh + 1) * chunk]
            rg = r_ref[r, vs].astype(F32)
            o_ref[r, vs] = (_rms(o, g_ref[...], SUBLN_EPS) * (rg * jax.nn.sigmoid(rg))).astype(BF16)

    if carry:
        @pl.when(blk == pl.num_programs(1) - 1)
        def _():
            s_ref[...] = st_ref[...].T


def _gla(q, k, v, la, rg, s0, s0_layer, g, layer, chunk):
    nb = s0.shape[1]
    t = q.shape[0] // nb
    carry = t > chunk
    if carry:
        tb = min(GLA_BLOCK, t)
        grid = (nb, t // tb)
        state = pl.BlockSpec((None, B_KEYS, B_VAL_DIM), lambda b, i: (b, 0, 0))
        state_in = pl.BlockSpec((None, None, B_KEYS, B_VAL_DIM), lambda b, i: (s0_layer, b, 0, 0))
    else:
        assert t == chunk
        streams = math.gcd(nb, max(1, GLA_BLOCK // t))
        tb = streams * t
        grid = (nb // streams, 1)
        state = pl.BlockSpec((streams, B_KEYS, B_VAL_DIM), lambda b, i: (b, 0, 0))
        state_in = pl.BlockSpec((None, streams, B_KEYS, B_VAL_DIM), lambda b, i: (s0_layer, b, 0, 0))
    nblk = grid[1]
    tok = lambda n: pl.BlockSpec((tb, n), lambda b, i: (b * nblk + i, 0))
    return pl.pallas_call(
        functools.partial(_gla_kernel, chunk=chunk, carry=carry),
        grid=grid,
        in_specs=[tok(B_KEYS), tok(B_KEYS), tok(B_WIDTH), tok(B_KEYS), tok(B_WIDTH), state_in, _pick(g, layer)],
        out_specs=[tok(B_WIDTH), state],
        out_shape=[jax.ShapeDtypeStruct((nb * t, B_WIDTH), BF16),
                   jax.ShapeDtypeStruct((nb, B_KEYS, B_VAL_DIM), F32)],
        scratch_shapes=[pltpu.VMEM((B_VAL_DIM, B_KEYS), F32)],
        compiler_params=_params("parallel", "arbitrary"),
        name="gla",
    )(q, k, v, la, rg, s0, g)


def _merge_kernel(x_ref, oa_ref, ob_ref, gpre_ref, gpost_ref, wgab_ref, woa_ref, wob_ref, wo_ref, o_ref):
    tm, d = x_ref.shape
    halves = [slice(i * (tm // MERGE_SPLIT), (i + 1) * (tm // MERGE_SPLIT)) for i in range(MERGE_SPLIT)]
    nt = (((1,), (1,)), ((), ()))
    first = []
    for r in halves:
        h = _rms(x_ref[r, :], gpre_ref[...], NORM_EPS).astype(BF16)
        first.append((lax.dot_general(h, wgab_ref[0], nt, preferred_element_type=F32),
                      jnp.dot(oa_ref[r, :], woa_ref[...], preferred_element_type=F32),
                      jnp.dot(ob_ref[r, :], wob_ref[...], preferred_element_type=F32)))
    mixed = []
    for gab, ya, yb in first:
        m = jax.nn.sigmoid(gab[:, :d]) * ya + jax.nn.sigmoid(gab[:, d:]) * yb
        mixed.append(jnp.dot(m.astype(BF16), wo_ref[...], preferred_element_type=F32))
    for r, y in zip(halves, mixed):
        o_ref[r, :] = x_ref[r, :] + _rms(y, gpost_ref[...], NORM_EPS)


def _merge(x, oa, ob, norm, w_t, woa, wob, wo, layer):
    t, d = x.shape
    tm = min(MERGE_ROW_TILE, t)
    row = lambda n: pl.BlockSpec((tm, n), lambda i: (i, 0))
    gate_rows = w_t.shape[1] - 2 * d
    w_gates = pl.BlockSpec((pl.Element(1), pl.Element(2 * d), pl.Element(d)), lambda i: (layer, gate_rows, 0),
                           pipeline_mode=pl.Buffered(1))
    return pl.pallas_call(
        _merge_kernel,
        grid=(t // tm,),
        in_specs=[row(d), row(A_WIDTH), row(B_WIDTH), _pick(norm, layer, 2), _pick(norm, layer, 3),
                  w_gates, _pick(woa, layer), _pick(wob, layer), _pick(wo, layer)],
        out_specs=row(d),
        out_shape=jax.ShapeDtypeStruct((t, d), F32),
        compiler_params=_params("parallel"),
        name="mixer_merge",
    )(x, oa, ob, norm, norm, w_t, woa, wob, wo)


def _rope_tables(pos):
    half = A_HEAD_DIM // 2
    inv = jnp.exp(-math.log(ROPE_THETA) * jnp.arange(half, dtype=F32) / half)
    ang = pos.astype(F32)[:, None] * inv[None, :]
    cos, sin = jnp.cos(ang), jnp.sin(ang)
    reps = LANES // A_HEAD_DIM
    return (jnp.tile(jnp.concatenate([cos, cos], axis=-1), (1, reps)),
            jnp.tile(jnp.concatenate([-sin, sin], axis=-1), (1, reps)))


def kernel(x_prompt, x_sample, cache_k, cache_v, state_gla, norm_g, ffn_w_gate, ffn_w_up, ffn_w_down, w_in,
           w_gate_up, b_gate, lambda_p, subln_g, gla_norm_g, w_out_a, w_out_b, w_out):
    depth = w_in.shape[0]
    bp, tp, d = x_prompt.shape
    bs, ts, _ = x_sample.shape
    past = cache_k.shape[2]
    assert bp == 1 and tp % CHUNK == 0

    n_qk = 2 * A_HEADS * A_HEAD_DIM
    assert w_in.shape[2] == 2 * n_qk + A_WIDTH + 2 * B_KEYS + 2 * B_WIDTH + GATE_RANK + 2 * d

    norm = norm_g.reshape(depth, norm_g.shape[1], 1, d)
    wg, wu, wd = ffn_w_gate.astype(BF16), ffn_w_up.astype(BF16), ffn_w_down.astype(BF16)
    w_t = jnp.transpose(w_in, (0, 2, 1)).astype(BF16)
    w_gu = w_gate_up.astype(BF16)
    bg = b_gate[:, None, :]
    woa, wob, wo = w_out_a.astype(BF16), w_out_b.astype(BF16), w_out.astype(BF16)
    sg = subln_g[:, None, :]
    gg = gla_norm_g[:, None, :]

    cos_p, sin_p = _rope_tables(jnp.arange(tp))
    cos_s, sin_s = _rope_tables(jnp.tile(past + jnp.arange(ts), bs))

    cache_kt = jnp.transpose(cache_k, (0, 1, 3, 4, 2))
    cache_vr = cache_v.reshape(depth, bs, past * A_HEADS, A_V_DIM)
    s0_sample = state_gla.reshape(depth, bs, B_KEYS, B_VAL_DIM)
    s0_prompt = jnp.zeros((1, bp, B_KEYS, B_VAL_DIM), F32)

    xp = x_prompt.reshape(tp, d)
    xs = x_sample.reshape(bs * ts, d)
    kv_p = kv_s = None
    states_p, states_s = [], []
    for l in range(depth):
        lam_init = 0.8 - 0.6 * math.exp(-0.3 * l)

        def layer(x, cos, sin, chunk, s0, s0_layer, kv, is_prompt):
            x = _ffn(x, norm, wg, wu, wd, l, 0)
            kf, vf, q1, q2, kb, vt, qg, kg, vg, rg, la = _proj(x, norm, w_t, w_gu, bg, cos, sin, l, is_prompt, kv)
            if is_prompt:
                oa = _attn_prompt(q1, q2, kb, vt, lambda_p, sg, l, lam_init)
            else:
                oa = _attn_sample(q1, q2, kb, vf, cache_kt, cache_vr, l, lambda_p, sg, lam_init)
            ob, s_new = _gla(qg, kg, vg, la, rg, s0, s0_layer, gg, l, chunk)
            x = _merge(x, oa, ob, norm, w_t, woa, wob, wo, l)
            x = _ffn(x, norm, wg, wu, wd, l, 1)
            return x, (kf, vf), s_new

        xp, kv_p, sp = layer(xp, cos_p, sin_p, CHUNK, s0_prompt, 0, kv_p, True)
        xs, kv_s, sn = layer(xs, cos_s, sin_s, ts, s0_sample, l, kv_s, False)
        states_p.append(sp)
        states_s.append(sn)

    k_prompt = jnp.transpose(kv_p[0].reshape(depth, bp, 2 * A_HEADS, A_HEAD_DIM, tp), (0, 1, 4, 2, 3))
    v_prompt = kv_p[1].reshape(depth, bp, tp, A_HEADS, A_V_DIM)
    s_prompt = jnp.stack(states_p).reshape(depth, bp, B_HEADS, B_KEY_DIM, B_VAL_DIM)
    k_sample = kv_s[0].reshape(depth, bs, ts, 2 * A_HEADS, A_HEAD_DIM)
    v_sample = kv_s[1].reshape(depth, bs, ts, A_HEADS, A_V_DIM)
    s_sample = jnp.stack(states_s).reshape(depth, bs, B_HEADS, B_KEY_DIM, B_VAL_DIM)
    return (xp.reshape(bp, tp, d), xs.reshape(bs, ts, d), k_prompt, v_prompt, s_prompt,
            k_sample, v_sample, s_sample)
```
